```python
import math
import jax, jax.numpy as jnp
from jax import lax
import numpy as np

D_MODEL = 1024
BATCH = 32
SEQ = 2048
DEPTH = 4
DEC_BATCH = 1
DEC_SEQ = 16384
PAST_LEN = 128

A_HEADS = 8
A_HEAD_DIM = 64
A_WIDTH = A_HEADS * A_HEAD_DIM
A_ROT_DIM = A_HEAD_DIM // 4
ATT_THETA = 500000.0
DILATED = ((128, 1), (512, 4), (2048, 16))
R_HEADS = 4
R_KEY_DIM = 128
R_VAL_DIM = 256
R_QK_WIDTH = R_HEADS * R_KEY_DIM
R_V_WIDTH = R_HEADS * R_VAL_DIM
RET_THETA = 10000.0
RET_CHUNK = 128
IN_SIZES = (A_WIDTH, A_WIDTH, A_WIDTH, A_WIDTH, R_QK_WIDTH, R_QK_WIDTH, R_V_WIDTH, R_V_WIDTH, D_MODEL, D_MODEL)
IN_WIDTH = sum(IN_SIZES)
IN_SPLIT_IDX = [int(v) for v in np.cumsum(IN_SIZES)[:-1]]
EPS = 1e-6
NEG = -1e30

kernel_name = "hybrid_dilated_attn_retention_encoder"


def rms_norm(x, g):
    xf = x.astype(jnp.float32)
    y = xf * lax.rsqrt(jnp.mean(xf * xf, axis=-1, keepdims=True) + EPS) * g.astype(jnp.float32)
    return y.astype(x.dtype)


def rope(x, pos, theta, n_rot):
    half = n_rot // 2
    freqs = jnp.exp(-math.log(theta) * jnp.arange(half, dtype=jnp.float32) / half)
    ang = pos[:, None] * freqs[None, :]
    cos = jnp.cos(ang)[:, None, :]
    sin = jnp.sin(ang)[:, None, :]
    x1 = x[..., :half].astype(jnp.float32)
    x2 = x[..., half:n_rot].astype(jnp.float32)
    rot = jnp.concatenate([x1 * cos - x2 * sin, x1 * sin + x2 * cos], axis=-1).astype(x.dtype)
    return jnp.concatenate([rot, x[..., n_rot:]], axis=-1)


def window_attn(q, k, v, dil, half):
    B, S, H, E = q.shape
    L = S // dil
    blk = half
    nb = -(-L // blk)
    Lp = nb * blk

    def strided(t):
        return t.reshape(B, L, dil, H, E).transpose(0, 2, 3, 1, 4)

    qs = jnp.pad(strided(q), ((0, 0), (0, 0), (0, 0), (0, Lp - L), (0, 0))).reshape(B, dil, H, nb, blk, E)

    def windows(t):
        tp = jnp.pad(strided(t), ((0, 0), (0, 0), (0, 0), (blk, Lp - L + blk), (0, 0))).reshape(B, dil, H, nb + 2, blk, E)
        return jnp.concatenate([tp[:, :, :, j:j + nb] for j in range(3)], axis=4)

    kw = windows(k)
    vw = windows(v)
    s = jnp.einsum('bdhnqe,bdhnke->bdhnqk', qs, kw, preferred_element_type=jnp.float32) * (E ** -0.5)
    qi = jnp.arange(blk)
    kj = jnp.arange(3 * blk)
    rel = kj[None, :] - blk - qi[:, None]
    kpos = jnp.arange(nb)[:, None] * blk + kj[None, :] - blk
    mask = (jnp.abs(rel) <= half)[None] & ((kpos >= 0) & (kpos < L))[:, None, :]
    s = jnp.where(mask, s, NEG)
    m = jnp.max(s, axis=-1, keepdims=True)
    p = jnp.exp(s - m)
    den = jnp.sum(p, axis=-1, keepdims=True)
    o = jnp.einsum('bdhnqk,bdhnke->bdhnqe', p, vw.astype(jnp.float32)) / den
    lse = (m + jnp.log(den))[..., 0]
    o = o.reshape(B, dil, H, Lp, E)[:, :, :, :L].transpose(0, 3, 1, 2, 4).reshape(B, S, H, E)
    lse = lse.reshape(B, dil, H, Lp)[..., :L].transpose(0, 3, 1, 2).reshape(B, S, H)
    return o, lse


def retention_scan(q, k, v, log_g):
    B, S, H, Dk = q.shape
    Dv = v.shape[-1]
    C = RET_CHUNK
    N = S // C
    qc = q.reshape(B, N, C, H, Dk)
    kc = k.reshape(B, N, C, H, Dk)
    vc = v.reshape(B, N, C, H, Dv)
    pos = jnp.arange(C, dtype=jnp.float32)
    rel = pos[:, None] - pos[None, :]
    intra = jnp.where(rel[None] >= 0, jnp.exp(jnp.maximum(rel, 0.0)[None] * log_g[:, None, None]), 0.0)
    scores = jnp.einsum('bnihd,bnjhd->bnhij', qc, kc) * intra
    inner = jnp.einsum('bnhij,bnjhe->bnihe', scores, vc)
    k_dec = jnp.exp((C - 1 - pos)[:, None] * log_g[None, :])
    q_dec = jnp.exp((pos + 1)[:, None] * log_g[None, :])
    delta = jnp.einsum('bnjhd,bnjhe->nbhde', kc * k_dec[:, :, None], vc)
    chunk_decay = jnp.exp(C * log_g)[None, :, None, None]

    def step(state, d):
        return state * chunk_decay + d, state

    _, prev = lax.scan(step, jnp.zeros((B, H, Dk, Dv), delta.dtype), delta)
    cross = jnp.einsum('bnihd,nbhde->bnihe', qc * q_dec[:, :, None], prev)
    return (inner + cross).reshape(B, S, H, Dv)


def mixer_layer(x, c, norm_g, w_ada, b_ada, w_in, q_norm_g, k_norm_g, decay_logit, ret_norm_g, w_proj_a, w_proj_b, w_out):
    B, S, _ = x.shape
    mod = jax.nn.silu(c) @ w_ada + b_ada
    shift, scale, gate = jnp.split(mod, 3, axis=-1)
    h = rms_norm(x, norm_g) * (1 + scale[:, None]) + shift[:, None]
    qa, ka, va, za, qr, kr, vr, zr, ga, gr = jnp.split(h @ w_in, IN_SPLIT_IDX, axis=-1)
    pos = jnp.arange(S, dtype=jnp.float32)

    qa = rope(rms_norm(qa.reshape(B, S, A_HEADS, A_HEAD_DIM), q_norm_g), pos, ATT_THETA, A_ROT_DIM)
    ka = rope(rms_norm(ka.reshape(B, S, A_HEADS, A_HEAD_DIM), k_norm_g), pos, ATT_THETA, A_ROT_DIM)
    va = va.reshape(B, S, A_HEADS, A_HEAD_DIM)
    outs = [window_attn(qa, ka, va, dil, win // (2 * dil)) for win, dil in DILATED]
    wts = jax.nn.softmax(jnp.stack([l for _, l in outs], axis=0), axis=0)
    ya = jnp.sum(wts[..., None] * jnp.stack([o for o, _ in outs], axis=0), axis=0)
    ya = (ya.reshape(B, S, A_WIDTH).astype(x.dtype) * jax.nn.silu(za)) @ w_proj_a

    qr = rope(qr.reshape(B, S, R_HEADS, R_KEY_DIM), pos, RET_THETA, R_KEY_DIM)
    kr = rope(kr.reshape(B, S, R_HEADS, R_KEY_DIM), pos, RET_THETA, R_KEY_DIM) * (R_KEY_DIM ** -0.5)
    vr = vr.reshape(B, S, R_HEADS, R_VAL_DIM)
    log_g = jax.nn.log_sigmoid(decay_logit.astype(jnp.float32))
    fwd = retention_scan(qr, kr, vr, log_g[0])
    bwd = jnp.flip(retention_scan(jnp.flip(qr, 1), jnp.flip(kr, 1), jnp.flip(vr, 1), log_g[1]), 1)
    yr = (fwd + bwd).astype(jnp.float32)
    mu = jnp.mean(yr, axis=-1, keepdims=True)
    var = jnp.mean(jnp.square(yr - mu), axis=-1, keepdims=True)
    yr = (yr - mu) * lax.rsqrt(var + EPS) * ret_norm_g.reshape(R_HEADS, R_VAL_DIM).astype(jnp.float32)
    yr = (yr.reshape(B, S, R_V_WIDTH).astype(x.dtype) * jax.nn.silu(zr)) @ w_proj_b

    merged = jax.nn.sigmoid(ga) * ya + jax.nn.sigmoid(gr) * yr
    return (x + gate[:, None] * (merged @ w_out)).astype(x.dtype)


def setup_inputs(seed: int = 0) -> dict:
    key = jax.random.key(seed)
    ks = jax.random.split(key, 16)
    f32 = jnp.float32
    base_logit = jnp.log(2.0 ** (5.0 + jnp.arange(R_HEADS, dtype=f32)) - 1.0)
    return {
        "x_prompt": jax.random.normal(ks[0], (BATCH, SEQ, D_MODEL), f32),
        "x_sample": jax.random.normal(ks[1], (DEC_BATCH, DEC_SEQ, D_MODEL), f32),
        "c_prompt": jax.random.normal(ks[2], (BATCH, D_MODEL), f32),
        "c_sample": jax.random.normal(ks[3], (DEC_BATCH, D_MODEL), f32),
        "norm_g": 1.0 + 0.02 * jax.random.normal(ks[4], (DEPTH, D_MODEL), f32),
        "w_ada": 0.5 * D_MODEL ** -0.5 * jax.random.normal(ks[5], (DEPTH, D_MODEL, 3 * D_MODEL), f32),
        "b_ada": 0.02 * jax.random.normal(ks[6], (DEPTH, 3 * D_MODEL), f32),
        "w_in": D_MODEL ** -0.5 * jax.random.normal(ks[7], (DEPTH, D_MODEL, IN_WIDTH), f32),
        "q_norm_g": 1.0 + 0.02 * jax.random.normal(ks[8], (DEPTH, A_HEAD_DIM), f32),
        "k_norm_g": 1.0 + 0.02 * jax.random.normal(ks[9], (DEPTH, A_HEAD_DIM), f32),
        "ret_decay_logit": base_logit[None, None, :] + 0.1 * jax.random.normal(ks[10], (DEPTH, 2, R_HEADS), f32),
        "ret_norm_g": 1.0 + 0.02 * jax.random.normal(ks[11], (DEPTH, R_V_WIDTH), f32),
        "w_proj_a": A_WIDTH ** -0.5 * jax.random.normal(ks[12], (DEPTH, A_WIDTH, D_MODEL), f32),
        "w_proj_b": R_V_WIDTH ** -0.5 * jax.random.normal(ks[13], (DEPTH, R_V_WIDTH, D_MODEL), f32),
        "w_out": D_MODEL ** -0.5 * jax.random.normal(ks[14], (DEPTH, D_MODEL, D_MODEL), f32),
    }


def reference(x_prompt, x_sample, c_prompt, c_sample, norm_g, w_ada, b_ada, w_in, q_norm_g, k_norm_g,
              ret_decay_logit, ret_norm_g, w_proj_a, w_proj_b, w_out):
    y_prompt = x_prompt
    y_sample = x_sample
    for l in range(DEPTH):
        params = (norm_g[l], w_ada[l], b_ada[l], w_in[l], q_norm_g[l], k_norm_g[l], ret_decay_logit[l],
                  ret_norm_g[l], w_proj_a[l], w_proj_b[l], w_out[l])
        y_prompt = mixer_layer(y_prompt, c_prompt, *params)
        y_sample = mixer_layer(y_sample, c_sample, *params)
    return (y_prompt, y_sample)
```

```python
import functools
import math

import jax
import jax.numpy as jnp
from jax import lax
from jax.experimental import pallas as pl
from jax.experimental.pallas import tpu as pltpu

D_MODEL = 1024
DEPTH = 4
A_HEADS = 8
A_HEAD_DIM = 64
A_WIDTH = A_HEADS * A_HEAD_DIM
A_ROT_HALF = A_HEAD_DIM // 8
ATT_THETA = 500000.0
DILATIONS = (1, 4, 16)
HALF_WIN = 64
R_HEADS = 4
R_KEY_DIM = 128
R_VAL_DIM = 256
R_V_WIDTH = R_HEADS * R_VAL_DIM
RET_THETA = 10000.0
RET_CHUNK = 128
IN_WIDTH = 4 * A_WIDTH + 2 * R_HEADS * R_KEY_DIM + 2 * R_V_WIDTH + 2 * D_MODEL
EPS = 1e-6
NEG = -1e30

LANES = 128
COL = 512
N_COL = IN_WIDTH // COL
CB_QA, CB_KA, CB_VA, CB_ZA, CB_QR, CB_KR = 0, 1, 2, 3, 4, 5
OFF_VR, OFF_ZR, OFF_GA, OFF_GR = 3072, 4096, 5120, 6144
ROPE_TABLE_W = 5 * LANES

TM_IN = 512
TM_OUT = 512
TQ_ATT = 512
Q_SUB = 128
K_WIN = Q_SUB + 2 * HALF_WIN
RET_BLOCK = 2048
VMEM_LIMIT = 56 * 1024 * 1024

F32 = jnp.float32
BF16 = jnp.bfloat16


def _silu(v):
    return v * jax.nn.sigmoid(v)


def _params(sem, vmem=VMEM_LIMIT):
    return pltpu.CompilerParams(dimension_semantics=sem, vmem_limit_bytes=vmem)


def _mod_kernel(c_ref, w_ref, b_ref, o_ref):
    s = _silu(c_ref[...])
    o_ref[...] = jnp.dot(s, w_ref[...], preferred_element_type=F32, precision=lax.Precision.HIGHEST) + b_ref[...]


def _modulation(c_all, w_ada, b_ada):
    nb = c_all.shape[0]
    return pl.pallas_call(
        _mod_kernel,
        grid=(DEPTH, 3),
        in_specs=[
            pl.BlockSpec((nb, D_MODEL), lambda l, j: (0, 0)),
            pl.BlockSpec((None, D_MODEL, D_MODEL), lambda l, j: (l, 0, j)),
            pl.BlockSpec((None, 1, D_MODEL), lambda l, j: (l, 0, j)),
        ],
        out_specs=pl.BlockSpec((None, nb, D_MODEL), lambda l, j: (l, 0, j)),
        out_shape=jax.ShapeDtypeStruct((DEPTH, nb, 3 * D_MODEL), F32),
        compiler_params=_params(("parallel", "parallel")),
        name="adaln_mod",
    )(c_all, w_ada, b_ada.reshape(DEPTH, 1, 3 * D_MODEL))


def _rope_tables(seq):
    pos = jnp.arange(seq, dtype=F32)
    fa = jnp.exp(-math.log(ATT_THETA) * jnp.arange(A_ROT_HALF, dtype=F32) / A_ROT_HALF)
    ang = pos[:, None] * fa[None, :]
    cos, sin = jnp.cos(ang), jnp.sin(ang)
    rest = A_HEAD_DIM - 2 * A_ROT_HALF
    one = jnp.ones((seq, rest), F32)
    zero = jnp.zeros((seq, rest), F32)
    z8 = jnp.zeros((seq, A_ROT_HALF), F32)
    att_c = jnp.tile(jnp.concatenate([cos, cos, one], 1), (1, 2))
    att_a = jnp.tile(jnp.concatenate([-sin, z8, zero], 1), (1, 2))
    att_b = jnp.tile(jnp.concatenate([z8, sin, zero], 1), (1, 2))
    half = R_KEY_DIM // 2
    fr = jnp.exp(-math.log(RET_THETA) * jnp.arange(half, dtype=F32) / half)
    ang = pos[:, None] * fr[None, :]
    cos, sin = jnp.cos(ang), jnp.sin(ang)
    ret_c = jnp.concatenate([cos, cos], 1)
    ret_s = jnp.concatenate([-sin, sin], 1)
    return jnp.concatenate([att_c, att_a, att_b, ret_c, ret_s], 1)


def _inproj_kernel(x_ref, mod_ref, g_ref, w_ref, gq_ref, gk_ref, e_ref, tab_ref, o_ref):
    x = x_ref[...]
    ms = jnp.mean(x * x, axis=-1, keepdims=True)
    y = x * lax.rsqrt(ms + EPS) * g_ref[...]
    h = (y * (1.0 + mod_ref[1:2, :]) + mod_ref[0:1, :]).astype(BF16)

    att_c = tab_ref[:, 0 * LANES:1 * LANES]
    att_a = tab_ref[:, 1 * LANES:2 * LANES]
    att_b = tab_ref[:, 2 * LANES:3 * LANES]
    ret_c = tab_ref[:, 3 * LANES:4 * LANES]
    ret_s = tab_ref[:, 4 * LANES:5 * LANES]

    for j in range(N_COL):
        acc = jnp.dot(h, w_ref[:, j * COL:(j + 1) * COL], preferred_element_type=F32)
        if j in (CB_QA, CB_KA):
            ss = jnp.dot((acc * acc).astype(BF16), e_ref[...], preferred_element_type=F32)
            gain = gq_ref[...] if j == CB_QA else gk_ref[...]
            yn = acc * lax.rsqrt(ss * (1.0 / A_HEAD_DIM) + EPS) * gain
            for c in range(COL // LANES):
                xc = yn[:, c * LANES:(c + 1) * LANES]
                rot = (xc * att_c + pltpu.roll(xc, LANES - A_ROT_HALF, 1) * att_a
                       + pltpu.roll(xc, A_ROT_HALF, 1) * att_b)
                o_ref[:, j * COL + c * LANES:j * COL + (c + 1) * LANES] = rot.astype(BF16)
        elif j in (CB_QR, CB_KR):
            for c in range(COL // LANES):
                xc = acc[:, c * LANES:(c + 1) * LANES]
                rot = xc * ret_c + pltpu.roll(xc, R_KEY_DIM // 2, 1) * ret_s
                if j == CB_KR:
                    rot = rot * (R_KEY_DIM ** -0.5)
                o_ref[:, j * COL + c * LANES:j * COL + (c + 1) * LANES] = rot.astype(BF16)
        else:
            o_ref[:, j * COL:(j + 1) * COL] = acc.astype(BF16)


def _inproj(x2, mod3, layer, b_off, seq, norm_g, w_in, gq, gk, e_bd, tables):
    t = x2.shape[0]
    per_seq = seq // TM_IN
    const = dict(pipeline_mode=pl.Buffered(1))
    return pl.pallas_call(
        _inproj_kernel,
        grid=(t // TM_IN,),
        in_specs=[
            pl.BlockSpec((TM_IN, D_MODEL), lambda i: (i, 0)),
            pl.BlockSpec((None, None, 3, D_MODEL), lambda i: (layer, i // per_seq + b_off, 0, 0)),
            pl.BlockSpec((1, D_MODEL), lambda i: (0, 0)),
            pl.BlockSpec((D_MODEL, IN_WIDTH), lambda i: (0, 0), **const),
            pl.BlockSpec((1, COL), lambda i: (0, 0)),
            pl.BlockSpec((1, COL), lambda i: (0, 0)),
            pl.BlockSpec((COL, COL), lambda i: (0, 0), **const),
            pl.BlockSpec((TM_IN, ROPE_TABLE_W), lambda i: (i % per_seq, 0)),
        ],
        out_specs=pl.BlockSpec((TM_IN, IN_WIDTH), lambda i: (i, 0)),
        out_shape=jax.ShapeDtypeStruct((t, IN_WIDTH), BF16),
        compiler_params=_params(("parallel",)),
        name="in_proj",
    )(x2, mod3, norm_g, w_in, gq, gk, e_bd, tables)


def _attn_kernel(q_ref, kp_ref, kc_ref, kn_ref, vp_ref, vc_ref, vn_ref, o_ref, lse_ref, kbuf, vbuf, *, tq, length):
    kbuf[0:HALF_WIN, :] = kp_ref[...]
    kbuf[HALF_WIN:HALF_WIN + tq, :] = kc_ref[...]
    kbuf[HALF_WIN + tq:, :] = kn_ref[...]
    vbuf[0:HALF_WIN, :] = vp_ref[...]
    vbuf[HALF_WIN:HALF_WIN + tq, :] = vc_ref[...]
    vbuf[HALF_WIN + tq:, :] = vn_ref[...]
    q0 = pl.program_id(2) * tq

    kj = lax.broadcasted_iota(jnp.int32, (Q_SUB, K_WIN), 1)
    qi = lax.broadcasted_iota(jnp.int32, (Q_SUB, K_WIN), 0)
    band = (kj >= qi) & (kj <= qi + 2 * HALF_WIN)
    lane = lax.broadcasted_iota(jnp.int32, (Q_SUB, LANES), 1)
    first_head = lane < A_HEAD_DIM

    def sub_block(s, carry):
        r0 = pl.multiple_of(s * Q_SUB, Q_SUB)
        kpos = kj + (q0 + r0 - HALF_WIN)
        mask = band & (kpos >= 0) & (kpos < length)
        lse_tile = jnp.zeros((Q_SUB, LANES), F32)
        for p in range(A_WIDTH // LANES):
            cols = slice(p * LANES, (p + 1) * LANES)
            q2 = q_ref[pl.ds(r0, Q_SUB), cols]
            k2 = kbuf[pl.ds(r0, K_WIN), cols]
            v2 = vbuf[pl.ds(r0, K_WIN), cols]
            o2 = jnp.zeros((Q_SUB, LANES), F32)
            for hh in range(2):
                mine = first_head if hh == 0 else jnp.logical_not(first_head)
                qm = jnp.where(mine, q2, jnp.zeros_like(q2))
                sc = lax.dot_general(qm, k2, (((1,), (1,)), ((), ())), preferred_element_type=F32)
                sc = jnp.where(mask, sc, NEG)
                m = jnp.max(sc, axis=-1, keepdims=True)
                pr = jnp.exp(sc - m)
                den = jnp.sum(pr, axis=-1, keepdims=True)
                pv = jnp.dot(pr.astype(BF16), v2, preferred_element_type=F32)
                o2 = jnp.where(mine, pv / den, o2)
                lse_tile = jnp.where(lane == 2 * p + hh, m + jnp.log(den), lse_tile)
            o_ref[pl.ds(r0, Q_SUB), cols] = o2.astype(BF16)
        lse_ref[pl.ds(r0, Q_SUB), :] = lse_tile
        return carry

    lax.fori_loop(0, tq // Q_SUB, sub_block, 0)


def _attention(p3, dil):
    b, seq, _ = p3.shape
    length = seq // dil
    tq = min(TQ_ATT, length)
    nq = length // tq
    halo_per_q = tq // HALF_WIN
    last_halo = length // HALF_WIN - 1
    pv = p3.reshape(b, length, dil * IN_WIDTH)
    ncb = N_COL

    def cur(cb):
        return pl.BlockSpec((None, tq, COL), lambda bi, r, i: (bi, i, r * ncb + cb))

    def prev(cb):
        return pl.BlockSpec((None, HALF_WIN, COL),
                            lambda bi, r, i: (bi, jnp.maximum(i * halo_per_q - 1, 0), r * ncb + cb))

    def nxt(cb):
        return pl.BlockSpec((None, HALF_WIN, COL),
                            lambda bi, r, i: (bi, jnp.minimum((i + 1) * halo_per_q, last_halo), r * ncb + cb))

    o, lse = pl.pallas_call(
        functools.partial(_attn_kernel, tq=tq, length=length),
        grid=(b, dil, nq),
        in_specs=[cur(CB_QA), prev(CB_KA), cur(CB_KA), nxt(CB_KA), prev(CB_VA), cur(CB_VA), nxt(CB_VA)],
        out_specs=[
            pl.BlockSpec((None, tq, A_WIDTH), lambda bi, r, i: (bi, i, r)),
            pl.BlockSpec((None, tq, LANES), lambda bi, r, i: (bi, i, r)),
        ],
        out_shape=[
            jax.ShapeDtypeStruct((b, length, dil * A_WIDTH), BF16),
            jax.ShapeDtypeStruct((b, length, dil * LANES), F32),
        ],
        scratch_shapes=[pltpu.VMEM((tq + 2 * HALF_WIN, COL), BF16), pltpu.VMEM((tq + 2 * HALF_WIN, COL), BF16)],
        compiler_params=_params(("parallel", "parallel", "parallel")),
        name=f"window_attn_d{dil}",
    )(pv, pv, pv, pv, pv, pv, pv)
    return o.reshape(b * seq, A_WIDTH), lse.reshape(b * seq, LANES)


def _log_sigmoid(v):
    return jnp.minimum(v, 0.0) - jnp.log1p(jnp.exp(-jnp.abs(v)))


def _decays(dl_ref, head):
    lg_f = _log_sigmoid(dl_ref[pl.ds(head, 1), :])
    lg_b = _log_sigmoid(dl_ref[pl.ds(head + R_HEADS, 1), :])
    return lg_f, lg_b


def _chunk_tables(lg_f, lg_b):
    c = RET_CHUNK
    row = lax.broadcasted_iota(jnp.int32, (c, c), 0).astype(F32)
    col = lax.broadcasted_iota(jnp.int32, (c, c), 1).astype(F32)
    rel = row - col
    intra = (jnp.where(rel >= 0, jnp.exp(jnp.maximum(rel, 0.0) * lg_f), 0.0)
             + jnp.where(rel <= 0, jnp.exp(jnp.maximum(-rel, 0.0) * lg_b), 0.0))
    kd_f = jnp.exp((c - 1 - row) * lg_f)
    kd_b = jnp.exp(row * lg_b)
    qd_f = jnp.exp((row + 1.0) * lg_f)
    qd_b = jnp.exp((c - row) * lg_b)
    cd_f = jnp.exp(c * lg_f)
    cd_b = jnp.exp(c * lg_b)
    return intra, kd_f, kd_b, qd_f, qd_b, cd_f, cd_b


def _wide(v):
    return jnp.concatenate([v, v], axis=-1)


def _delta(k_chunk, kd, v_chunk):
    kt = (k_chunk.astype(F32) * kd).T.astype(BF16)
    return jnp.dot(kt, v_chunk, preferred_element_type=F32)


def _ret_state_kernel(dl_ref, kf_ref, vf_ref, kb_ref, vb_ref, fin_ref, bin_ref, sf, sb):
    head = pl.program_id(0)

    @pl.when(pl.program_id(1) == 0)
    def _():
        sf[...] = jnp.zeros_like(sf)
        sb[...] = jnp.zeros_like(sb)

    fin_ref[...] = sf[...]
    bin_ref[...] = sb[...]
    lg_f, lg_b = _decays(dl_ref, head)
    _, kd_f, kd_b, _, _, cd_f, cd_b = _chunk_tables(lg_f, lg_b)
    cd_f, cd_b = _wide(cd_f), _wide(cd_b)
    n_chunks = RET_BLOCK // RET_CHUNK

    def fwd(n, carry):
        r0 = pl.multiple_of(n * RET_CHUNK, RET_CHUNK)
        sf[...] = sf[...] * cd_f + _delta(kf_ref[pl.ds(r0, RET_CHUNK), :], kd_f, vf_ref[pl.ds(r0, RET_CHUNK), :])
        return carry

    def bwd(t, carry):
        r0 = pl.multiple_of((n_chunks - 1 - t) * RET_CHUNK, RET_CHUNK)
        sb[...] = sb[...] * cd_b + _delta(kb_ref[pl.ds(r0, RET_CHUNK), :], kd_b, vb_ref[pl.ds(r0, RET_CHUNK), :])
        return carry

    lax.fori_loop(0, n_chunks, fwd, 0)
    lax.fori_loop(0, n_chunks, bwd, 0)


def _retention_states(p3, dl_tile):
    nb = p3.shape[0]
    k_cb = (CB_KR * COL) // R_KEY_DIM
    v_cb = OFF_VR // R_VAL_DIM
    shape = jax.ShapeDtypeStruct((nb, R_HEADS, R_KEY_DIM, R_VAL_DIM), F32)
    return pl.pallas_call(
        _ret_state_kernel,
        grid=(R_HEADS, nb),
        in_specs=[
            pl.BlockSpec((2 * R_HEADS, LANES), lambda h, j: (0, 0)),
            pl.BlockSpec((None, RET_BLOCK, R_KEY_DIM), lambda h, j: (j, 0, k_cb + h)),
            pl.BlockSpec((None, RET_BLOCK, R_VAL_DIM), lambda h, j: (j, 0, v_cb + h)),
            pl.BlockSpec((None, RET_BLOCK, R_KEY_DIM), lambda h, j: (nb - 1 - j, 0, k_cb + h)),
            pl.BlockSpec((None, RET_BLOCK, R_VAL_DIM), lambda h, j: (nb - 1 - j, 0, v_cb + h)),
        ],
        out_specs=[
            pl.BlockSpec((None, None, R_KEY_DIM, R_VAL_DIM), lambda h, j: (j, h, 0, 0)),
            pl.BlockSpec((None, None, R_KEY_DIM, R_VAL_DIM), lambda h, j: (nb - 1 - j, h, 0, 0)),
        ],
        out_shape=[shape, shape],
        scratch_shapes=[pltpu.VMEM((R_KEY_DIM, R_VAL_DIM), F32), pltpu.VMEM((R_KEY_DIM, R_VAL_DIM), F32)],
        compiler_params=_params(("parallel", "arbitrary")),
        name="retention_states",
    )(dl_tile, p3, p3, p3, p3)


def _ret_kernel(*refs, has_state):
    if has_state:
        dl_ref, gn_ref, q_ref, k_ref, v_ref, z_ref, fin_ref, bin_ref, o_ref, sf, sb, cross_b = refs
    else:
        dl_ref, gn_ref, q_ref, k_ref, v_ref, z_ref, o_ref, sf, sb, cross_b = refs
    head = pl.program_id(1)
    lg_f, lg_b = _decays(dl_ref, head)
    intra, kd_f, kd_b, qd_f, qd_b, cd_f, cd_b = _chunk_tables(lg_f, lg_b)
    cd_f, cd_b = _wide(cd_f), _wide(cd_b)
    qd_f, qd_b = _wide(qd_f), _wide(qd_b)
    gn = gn_ref[...]
    if has_state:
        sf[...] = fin_ref[...]
        sb[...] = bin_ref[...]
    else:
        sf[...] = jnp.zeros_like(sf)
        sb[...] = jnp.zeros_like(sb)
    n_chunks = RET_BLOCK // RET_CHUNK

    def bwd(t, carry):
        r0 = pl.multiple_of((n_chunks - 1 - t) * RET_CHUNK, RET_CHUNK)
        rows = pl.ds(r0, RET_CHUNK)
        state = sb[...]
        cross_b[rows, :] = jnp.dot(q_ref[rows, :], state.astype(BF16), preferred_element_type=F32) * qd_b
        sb[...] = state * cd_b + _delta(k_ref[rows, :], kd_b, v_ref[rows, :])
        return carry

    def fwd(n, carry):
        r0 = pl.multiple_of(n * RET_CHUNK, RET_CHUNK)
        rows = pl.ds(r0, RET_CHUNK)
        q, k, v = q_ref[rows, :], k_ref[rows, :], v_ref[rows, :]
        state = sf[...]
        sc = lax.dot_general(q, k, (((1,), (1,)), ((), ())), preferred_element_type=F32) * intra
        y = jnp.dot(sc.astype(BF16), v, preferred_element_type=F32)
        y = y + jnp.dot(q, state.astype(BF16), preferred_element_type=F32) * qd_f + cross_b[rows, :]
        mu = jnp.mean(y, axis=-1, keepdims=True)
        yc = y - mu
        var = jnp.mean(yc * yc, axis=-1, keepdims=True)
        yn = yc * lax.rsqrt(var + EPS) * gn
        o_ref[rows, :] = (yn * _silu(z_ref[rows, :].astype(F32))).astype(BF16)
        sf[...] = state * cd_f + _delta(k, kd_f, v)
        return carry

    lax.fori_loop(0, n_chunks, bwd, 0)
    lax.fori_loop(0, n_chunks, fwd, 0)


def _retention(p3, dl_tile, gn, states):
    nb = p3.shape[0]
    q_cb = (CB_QR * COL) // R_KEY_DIM
    k_cb = (CB_KR * COL) // R_KEY_DIM
    v_cb = OFF_VR // R_VAL_DIM
    z_cb = OFF_ZR // R_VAL_DIM
    in_specs = [
        pl.BlockSpec((2 * R_HEADS, LANES), lambda b, h: (0, 0)),
        pl.BlockSpec((1, R_VAL_DIM), lambda b, h: (0, h)),
        pl.BlockSpec((None, RET_BLOCK, R_KEY_DIM), lambda b, h: (b, 0, q_cb + h)),
        pl.BlockSpec((None, RET_BLOCK, R_KEY_DIM), lambda b, h: (b, 0, k_cb + h)),
        pl.BlockSpec((None, RET_BLOCK, R_VAL_DIM), lambda b, h: (b, 0, v_cb + h)),
        pl.BlockSpec((None, RET_BLOCK, R_VAL_DIM), lambda b, h: (b, 0, z_cb + h)),
    ]
    args = [dl_tile, gn, p3, p3, p3, p3]
    if states is not None:
        st = pl.BlockSpec((None, None, R_KEY_DIM, R_VAL_DIM), lambda b, h: (b, h, 0, 0))
        in_specs += [st, st]
        args += list(states)
    out = pl.pallas_call(
        functools.partial(_ret_kernel, has_state=states is not None),
        grid=(nb, R_HEADS),
        in_specs=in_specs,
        out_specs=pl.BlockSpec((None, RET_BLOCK, R_VAL_DIM), lambda b, h: (b, 0, h)),
        out_shape=jax.ShapeDtypeStruct((nb, RET_BLOCK, R_V_WIDTH), BF16),
        scratch_shapes=[
            pltpu.VMEM((R_KEY_DIM, R_VAL_DIM), F32),
            pltpu.VMEM((R_KEY_DIM, R_VAL_DIM), F32),
            pltpu.VMEM((RET_BLOCK, R_VAL_DIM), F32),
        ],
        compiler_params=_params(("parallel", "parallel")),
        name="retention",
    )(*args)
    return out.reshape(nb * RET_BLOCK, R_V_WIDTH)


def _out_kernel(x_ref, mod_ref, o1_ref, o2_ref, o3_ref, l1_ref, l2_ref, l3_ref, za_ref, ur_ref, ga_ref, gr_ref,
                wa_ref, wb_ref, wo_ref, y_ref):
    lses = (l1_ref[...], l2_ref[...], l3_ref[...])
    top = jnp.maximum(jnp.maximum(lses[0], lses[1]), lses[2])
    es = [jnp.exp(v - top) for v in lses]
    den = es[0] + es[1] + es[2]
    ws = [e / den for e in es]
    o_refs = (o1_ref, o2_ref, o3_ref)
    lane = lax.broadcasted_iota(jnp.int32, (x_ref.shape[0], LANES), 1)
    first_head = lane < A_HEAD_DIM
    ua = []
    for p in range(A_WIDTH // LANES):
        cols = slice(p * LANES, (p + 1) * LANES)
        ya = jnp.zeros((x_ref.shape[0], LANES), F32)
        for g in range(3):
            w_pair = jnp.where(first_head, ws[g][:, 2 * p:2 * p + 1], ws[g][:, 2 * p + 1:2 * p + 2])
            ya = ya + w_pair * o_refs[g][:, cols].astype(F32)
        ua.append((ya * _silu(za_ref[:, cols].astype(F32))).astype(BF16))
    ua = jnp.concatenate(ua, axis=-1)
    ya_p = jnp.dot(ua, wa_ref[...], preferred_element_type=F32)
    yr_p = jnp.dot(ur_ref[...], wb_ref[...], preferred_element_type=F32)
    merged = jax.nn.sigmoid(ga_ref[...].astype(F32)) * ya_p + jax.nn.sigmoid(gr_ref[...].astype(F32)) * yr_p
    out = jnp.dot(merged.astype(BF16), wo_ref[...], preferred_element_type=F32)
    y_ref[...] = x_ref[...] + mod_ref[2:3, :] * out


def _output(x2, mod3, layer, b_off, seq, attn, p2, ur, wa, wb, wo):
    t = x2.shape[0]
    per_seq = seq // TM_OUT
    const = dict(pipeline_mode=pl.Buffered(1))
    (o1, l1), (o2, l2), (o3, l3) = attn

    def rows(width, cb=0):
        return pl.BlockSpec((TM_OUT, width), lambda i: (i, cb))

    return pl.pallas_call(
        _out_kernel,
        grid=(t // TM_OUT,),
        in_specs=[
            rows(D_MODEL),
            pl.BlockSpec((None, None, 3, D_MODEL), lambda i: (layer, i // per_seq + b_off, 0, 0)),
            rows(A_WIDTH), rows(A_WIDTH), rows(A_WIDTH),
            rows(LANES), rows(LANES), rows(LANES),
            rows(A_WIDTH, CB_ZA),
            rows(R_V_WIDTH),
            rows(D_MODEL, OFF_GA // D_MODEL),
            rows(D_MODEL, OFF_GR // D_MODEL),
            pl.BlockSpec((A_WIDTH, D_MODEL), lambda i: (0, 0), **const),
            pl.BlockSpec((R_V_WIDTH, D_MODEL), lambda i: (0, 0), **const),
            pl.BlockSpec((D_MODEL, D_MODEL), lambda i: (0, 0), **const),
        ],
        out_specs=rows(D_MODEL),
        out_shape=jax.ShapeDtypeStruct((t, D_MODEL), F32),
        compiler_params=_params(("parallel",)),
        name="merge_out_proj",
    )(x2, mod3, o1, o2, o3, l1, l2, l3, p2, ur, p2, p2, wa, wb, wo)


def _mixer_layer(x2, batch, seq, mod3, layer, b_off, lw, tables, e_bd):
    p2 = _inproj(x2, mod3, layer, b_off, seq, lw["norm_g"], lw["w_in"], lw["gq"], lw["gk"], e_bd, tables)
    p3 = p2.reshape(batch, seq, IN_WIDTH)
    attn = [_attention(p3, dil) for dil in DILATIONS]
    nb = batch * seq // RET_BLOCK
    pr = p2.reshape(nb, RET_BLOCK, IN_WIDTH)
    states = _retention_states(pr, lw["dl"]) if seq > RET_BLOCK else None
    ur = _retention(pr, lw["dl"], lw["gn"], states)
    return _output(x2, mod3, layer, b_off, seq, attn, p2, ur, lw["wa"], lw["wb"], lw["wo"])


def _layer_weights(layer, norm_g, w_in, q_norm_g, k_norm_g, ret_decay_logit, ret_norm_g, w_proj_a, w_proj_b, w_out):
    return {
        "norm_g": norm_g[layer].reshape(1, D_MODEL),
        "w_in": w_in[layer].astype(BF16),
        "gq": (jnp.tile(q_norm_g[layer], A_HEADS) * (A_HEAD_DIM ** -0.5)).reshape(1, A_WIDTH),
        "gk": jnp.tile(k_norm_g[layer], A_HEADS).reshape(1, A_WIDTH),
        "dl": jnp.broadcast_to(ret_decay_logit[layer].astype(F32).reshape(2 * R_HEADS, 1), (2 * R_HEADS, LANES)),
        "gn": ret_norm_g[layer].reshape(1, R_V_WIDTH),
        "wa": w_proj_a[layer].astype(BF16),
        "wb": w_proj_b[layer].astype(BF16),
        "wo": w_out[layer].astype(BF16),
    }


def kernel(x_prompt, x_sample, c_prompt, c_sample, norm_g, w_ada, b_ada, w_in, q_norm_g, k_norm_g, ret_decay_logit,
           ret_norm_g, w_proj_a, w_proj_b, w_out):
    bp, sp, _ = x_prompt.shape
    bs, ss, _ = x_sample.shape
    pad = (-(bp + bs)) % 8
    c_all = jnp.concatenate([c_prompt, c_sample, jnp.zeros((pad, D_MODEL), F32)], axis=0)
    mod3 = _modulation(c_all, w_ada, b_ada).reshape(DEPTH, bp + bs + pad, 3, D_MODEL)
    head_of_lane = jnp.arange(A_WIDTH) // A_HEAD_DIM
    e_bd = (head_of_lane[:, None] == head_of_lane[None, :]).astype(BF16)
    tab_p = _rope_tables(sp)
    tab_s = tab_p if ss == sp else _rope_tables(ss)
    yp = x_prompt.reshape(bp * sp, D_MODEL)
    ys = x_sample.reshape(bs * ss, D_MODEL)
    for layer in range(DEPTH):
        lw = _layer_weights(layer, norm_g, w_in, q_norm_g, k_norm_g, ret_decay_logit, ret_norm_g, w_proj_a, w_proj_b,
                            w_out)
        yp = _mixer_layer(yp, bp, sp, mod3, layer, 0, lw, tab_p, e_bd)
        ys = _mixer_layer(ys, bs, ss, mod3, layer, bp, lw, tab_s, e_bd)
    return yp.reshape(bp, sp, D_MODEL), ys.reshape(bs, ss, D_MODEL)
```

```python
import functools
import math

import jax
import jax.numpy as jnp
from jax import lax
from jax.experimental import pallas as pl
from jax.experimental.pallas import tpu as pltpu

D_MODEL = 1024
DEPTH = 4
A_HEADS = 8
A_HEAD_DIM = 64
A_WIDTH = A_HEADS * A_HEAD_DIM
A_ROT_HALF = A_HEAD_DIM // 8
ATT_THETA = 500000.0
DILATIONS = (1, 4, 16)
HALF_WIN = 64
R_HEADS = 4
R_KEY_DIM = 128
R_VAL_DIM = 256
R_V_WIDTH = R_HEADS * R_VAL_DIM
RET_THETA = 10000.0
RET_CHUNK = 128
IN_WIDTH = 4 * A_WIDTH + 2 * R_HEADS * R_KEY_DIM + 2 * R_V_WIDTH + 2 * D_MODEL
EPS = 1e-6
NEG = -1e30
LOG2E = 1.4426950408889634
LN2 = 0.6931471805599453

LANES = 128
COL = 512
N_COL = IN_WIDTH // COL
N_CHUNK = COL // LANES
CB_QA, CB_KA, CB_VA, CB_ZA, CB_QR, CB_KR = 0, 1, 2, 3, 4, 5
OFF_VR, OFF_ZR, OFF_GA, OFF_GR = 3072, 4096, 5120, 6144
QKV_W = 3 * A_WIDTH
ROPE_TABLE_W = 5 * LANES

TM_IN = 512
TM_OUT = 512
TQ_ATT = 512
Q_SUB = 128
K_WIN = Q_SUB + 2 * HALF_WIN
RET_BLOCK = 2048
VMEM_LIMIT = 56 * 1024 * 1024

F32 = jnp.float32
BF16 = jnp.bfloat16


def _silu(v):
    return v * jax.nn.sigmoid(v)


def _params(sem, vmem=VMEM_LIMIT):
    return pltpu.CompilerParams(dimension_semantics=sem, vmem_limit_bytes=vmem)


def _mod_kernel(c_ref, w_ref, b_ref, o_ref):
    s = _silu(c_ref[...])
    o_ref[...] = jnp.dot(s, w_ref[...], preferred_element_type=F32, precision=lax.Precision.HIGHEST) + b_ref[...]


def _modulation(c_all, w_ada, b_ada):
    nb = c_all.shape[0]
    return pl.pallas_call(
        _mod_kernel,
        grid=(DEPTH, 3),
        in_specs=[
            pl.BlockSpec((nb, D_MODEL), lambda l, j: (0, 0)),
            pl.BlockSpec((None, D_MODEL, D_MODEL), lambda l, j: (l, 0, j)),
            pl.BlockSpec((None, 1, D_MODEL), lambda l, j: (l, 0, j)),
        ],
        out_specs=pl.BlockSpec((None, nb, D_MODEL), lambda l, j: (l, 0, j)),
        out_shape=jax.ShapeDtypeStruct((DEPTH, nb, 3 * D_MODEL), F32),
        compiler_params=_params(("parallel", "parallel")),
        name="adaln_mod",
    )(c_all, w_ada, b_ada.reshape(DEPTH, 1, 3 * D_MODEL))


def _rope_tables(seq):
    pos = jnp.arange(seq, dtype=F32)
    fa = jnp.exp(-math.log(ATT_THETA) * jnp.arange(A_ROT_HALF, dtype=F32) / A_ROT_HALF)
    ang = pos[:, None] * fa[None, :]
    cos, sin = jnp.cos(ang), jnp.sin(ang)
    rest = A_HEAD_DIM - 2 * A_ROT_HALF
    one = jnp.ones((seq, rest), F32)
    zero = jnp.zeros((seq, rest), F32)
    z8 = jnp.zeros((seq, A_ROT_HALF), F32)
    att_c = jnp.tile(jnp.concatenate([cos, cos, one], 1), (1, 2))
    att_a = jnp.tile(jnp.concatenate([-sin, z8, zero], 1), (1, 2))
    att_b = jnp.tile(jnp.concatenate([z8, sin, zero], 1), (1, 2))
    half = R_KEY_DIM // 2
    fr = jnp.exp(-math.log(RET_THETA) * jnp.arange(half, dtype=F32) / half)
    ang = pos[:, None] * fr[None, :]
    cos, sin = jnp.cos(ang), jnp.sin(ang)
    ret_c = jnp.concatenate([cos, cos], 1)
    ret_s = jnp.concatenate([-sin, sin], 1)
    return jnp.concatenate([att_c, att_a, att_b, ret_c, ret_s], 1)


def _inproj_kernel(x_ref, mod_ref, g_ref, w_ref, gq_ref, gk_ref, e_ref, tab_ref, o_ref, a4_ref, a16_ref, rows_scr):
    x = x_ref[...]
    ms = jnp.mean(x * x, axis=-1, keepdims=True)
    y = x * lax.rsqrt(ms + EPS) * g_ref[...]
    h = (y * (1.0 + mod_ref[1:2, :]) + mod_ref[0:1, :]).astype(BF16)

    att_c = tab_ref[:, 0 * LANES:1 * LANES]
    att_a = tab_ref[:, 1 * LANES:2 * LANES]
    att_b = tab_ref[:, 2 * LANES:3 * LANES]
    ret_c = tab_ref[:, 3 * LANES:4 * LANES]
    ret_s = tab_ref[:, 4 * LANES:5 * LANES]

    def emit_qkv(j, c, val):
        lo = j * COL + c * LANES
        o_ref[:, lo:lo + LANES] = val.astype(BF16)
        rows_scr[c] = val
        for dil, ref in ((4, a4_ref), (16, a16_ref)):
            n = TM_IN // dil
            for r in range(dil):
                ref[r, :, lo:lo + LANES] = rows_scr[c, pl.ds(r, n, stride=dil), :].astype(BF16)

    for j in range(N_COL):
        acc = jnp.dot(h, w_ref[:, j * COL:(j + 1) * COL], preferred_element_type=F32)
        if j in (CB_QA, CB_KA):
            ss = jnp.dot((acc * acc).astype(BF16), e_ref[...], preferred_element_type=F32)
            gain = gq_ref[...] if j == CB_QA else gk_ref[...]
            yn = acc * lax.rsqrt(ss * (1.0 / A_HEAD_DIM) + EPS) * gain
            for c in range(N_CHUNK):
                xc = yn[:, c * LANES:(c + 1) * LANES]
                rot = (xc * att_c + pltpu.roll(xc, LANES - A_ROT_HALF, 1) * att_a
                       + pltpu.roll(xc, A_ROT_HALF, 1) * att_b)
                emit_qkv(j, c, rot)
        elif j == CB_VA:
            for c in range(N_CHUNK):
                emit_qkv(j, c, acc[:, c * LANES:(c + 1) * LANES])
        elif j in (CB_QR, CB_KR):
            for c in range(N_CHUNK):
                xc = acc[:, c * LANES:(c + 1) * LANES]
                rot = xc * ret_c + pltpu.roll(xc, R_KEY_DIM // 2, 1) * ret_s
                if j == CB_KR:
                    rot = rot * (R_KEY_DIM ** -0.5)
                o_ref[:, j * COL + c * LANES:j * COL + (c + 1) * LANES] = rot.astype(BF16)
        else:
            o_ref[:, j * COL:(j + 1) * COL] = acc.astype(BF16)


def _inproj(x2, batch, seq, mod3, layer, b_off, norm_g, w_in, gq, gk, e_bd, tables):
    t = x2.shape[0]
    per_seq = seq // TM_IN
    const = dict(pipeline_mode=pl.Buffered(1))

    def dilated(dil):
        return pl.BlockSpec((None, dil, TM_IN // dil, QKV_W), lambda i: (i // per_seq, 0, i % per_seq, 0))

    return pl.pallas_call(
        _inproj_kernel,
        grid=(t // TM_IN,),
        in_specs=[
            pl.BlockSpec((TM_IN, D_MODEL), lambda i: (i, 0)),
            pl.BlockSpec((None, None, 3, D_MODEL), lambda i: (layer, i // per_seq + b_off, 0, 0)),
            pl.BlockSpec((1, D_MODEL), lambda i: (0, 0)),
            pl.BlockSpec((D_MODEL, IN_WIDTH), lambda i: (0, 0), **const),
            pl.BlockSpec((1, COL), lambda i: (0, 0)),
            pl.BlockSpec((1, COL), lambda i: (0, 0)),
            pl.BlockSpec((COL, COL), lambda i: (0, 0), **const),
            pl.BlockSpec((TM_IN, ROPE_TABLE_W), lambda i: (i % per_seq, 0)),
        ],
        out_specs=[pl.BlockSpec((TM_IN, IN_WIDTH), lambda i: (i, 0)), dilated(4), dilated(16)],
        out_shape=[
            jax.ShapeDtypeStruct((t, IN_WIDTH), BF16),
            jax.ShapeDtypeStruct((batch, 4, seq // 4, QKV_W), BF16),
            jax.ShapeDtypeStruct((batch, 16, seq // 16, QKV_W), BF16),
        ],
        scratch_shapes=[pltpu.VMEM((N_CHUNK, TM_IN, LANES), F32)],
        compiler_params=_params(("parallel",)),
        name="in_proj",
    )(x2, mod3, norm_g, w_in, gq, gk, e_bd, tables)


def _attn_kernel(q_ref, kp_ref, kc_ref, kn_ref, vp_ref, vc_ref, vn_ref, o_ref, lse_ref, kbuf, vbuf, *, tq, nr, length):
    kbuf[:, 0:HALF_WIN, :] = kp_ref[...]
    kbuf[:, HALF_WIN:HALF_WIN + tq, :] = kc_ref[...]
    kbuf[:, HALF_WIN + tq:, :] = kn_ref[...]
    vbuf[:, 0:HALF_WIN, :] = vp_ref[...]
    vbuf[:, HALF_WIN:HALF_WIN + tq, :] = vc_ref[...]
    vbuf[:, HALF_WIN + tq:, :] = vn_ref[...]
    lse_ref[...] = jnp.zeros_like(lse_ref)
    q0 = pl.program_id(2) * tq

    kj = lax.broadcasted_iota(jnp.int32, (Q_SUB, K_WIN), 1)
    qi = lax.broadcasted_iota(jnp.int32, (Q_SUB, K_WIN), 0)
    band = (kj >= qi) & (kj <= qi + 2 * HALF_WIN)
    lane = lax.broadcasted_iota(jnp.int32, (Q_SUB, LANES), 1)
    first_head = lane < A_HEAD_DIM
    sub_per_res = tq // Q_SUB

    def sub_block(t, carry):
        ri = t // sub_per_res
        r0 = pl.multiple_of((t % sub_per_res) * Q_SUB, Q_SUB)
        kpos = kj + (q0 + r0 - HALF_WIN)
        bias = jnp.where(band & (kpos >= 0) & (kpos < length), 0.0, NEG)
        bias = jnp.concatenate([bias, bias], axis=0)
        for p in range(A_WIDTH // LANES):
            cols = slice(p * LANES, (p + 1) * LANES)
            q2 = q_ref[ri, pl.ds(r0, Q_SUB), cols]
            k2 = kbuf[ri, pl.ds(r0, K_WIN), cols]
            v2 = vbuf[ri, pl.ds(r0, K_WIN), cols]
            zero = jnp.zeros_like(q2)
            qs = jnp.concatenate([jnp.where(first_head, q2, zero), jnp.where(first_head, zero, q2)], axis=0)
            sc = lax.dot_general(qs, k2, (((1,), (1,)), ((), ())), preferred_element_type=F32) + bias
            m = jnp.max(sc, axis=-1, keepdims=True)
            pr = jnp.exp2(sc - m)
            den = jnp.sum(pr, axis=-1, keepdims=True)
            pv = jnp.dot(pr.astype(BF16), v2, preferred_element_type=F32) / den
            lse = (m + jnp.log2(den)) * LN2
            lse_ref[ri, pl.ds(r0, Q_SUB), 2 * p:2 * p + 1] = lse[:Q_SUB]
            lse_ref[ri, pl.ds(r0, Q_SUB), 2 * p + 1:2 * p + 2] = lse[Q_SUB:]
            o_ref[ri, pl.ds(r0, Q_SUB), cols] = jnp.where(first_head, pv[:Q_SUB], pv[Q_SUB:]).astype(BF16)
        return carry

    lax.fori_loop(0, nr * sub_per_res, sub_block, 0)


def _attention(qkv, dil):
    b, _, length, _ = qkv.shape
    tq = min(TQ_ATT, length)
    nr = min(dil, TQ_ATT // tq)
    nq = length // tq
    halo_per_q = tq // HALF_WIN
    last_halo = length // HALF_WIN - 1

    def cur(cb):
        return pl.BlockSpec((None, nr, tq, COL), lambda bi, r, i: (bi, r, i, cb))

    def prev(cb):
        return pl.BlockSpec((None, nr, HALF_WIN, COL),
                            lambda bi, r, i: (bi, r, jnp.maximum(i * halo_per_q - 1, 0), cb))

    def nxt(cb):
        return pl.BlockSpec((None, nr, HALF_WIN, COL),
                            lambda bi, r, i: (bi, r, jnp.minimum((i + 1) * halo_per_q, last_halo), cb))

    return pl.pallas_call(
        functools.partial(_attn_kernel, tq=tq, nr=nr, length=length),
        grid=(b, dil // nr, nq),
        in_specs=[cur(CB_QA), prev(CB_KA), cur(CB_KA), nxt(CB_KA), prev(CB_VA), cur(CB_VA), nxt(CB_VA)],
        out_specs=[
            pl.BlockSpec((None, nr, tq, A_WIDTH), lambda bi, r, i: (bi, r, i, 0)),
            pl.BlockSpec((None, nr, tq, LANES), lambda bi, r, i: (bi, r, i, 0)),
        ],
        out_shape=[
            jax.ShapeDtypeStruct((b, dil, length, A_WIDTH), BF16),
            jax.ShapeDtypeStruct((b, dil, length, LANES), F32),
        ],
        scratch_shapes=[pltpu.VMEM((nr, tq + 2 * HALF_WIN, COL), BF16), pltpu.VMEM((nr, tq + 2 * HALF_WIN, COL), BF16)],
        compiler_params=_params(("parallel", "parallel", "parallel")),
        name=f"window_attn_d{dil}",
    )(qkv, qkv, qkv, qkv, qkv, qkv, qkv)


def _log_sigmoid(v):
    return jnp.minimum(v, 0.0) - jnp.log1p(jnp.exp(-jnp.abs(v)))


def _decays(dl_ref, head):
    lg_f = _log_sigmoid(dl_ref[pl.ds(head, 1), :])
    lg_b = _log_sigmoid(dl_ref[pl.ds(head + R_HEADS, 1), :])
    return lg_f, lg_b


def _chunk_tables(lg_f, lg_b):
    c = RET_CHUNK
    row = lax.broadcasted_iota(jnp.int32, (c, c), 0).astype(F32)
    col = lax.broadcasted_iota(jnp.int32, (c, c), 1).astype(F32)
    rel = row - col
    intra = (jnp.where(rel >= 0, jnp.exp(jnp.maximum(rel, 0.0) * lg_f), 0.0)
             + jnp.where(rel <= 0, jnp.exp(jnp.maximum(-rel, 0.0) * lg_b), 0.0))
    kd_f = jnp.exp((c - 1 - row) * lg_f)
    kd_b = jnp.exp(row * lg_b)
    qd_f = jnp.exp((row + 1.0) * lg_f)
    qd_b = jnp.exp((c - row) * lg_b)
    cd_f = jnp.exp(c * lg_f)
    cd_b = jnp.exp(c * lg_b)
    return intra, kd_f, kd_b, qd_f, qd_b, cd_f, cd_b


def _wide(v):
    return jnp.concatenate([v, v], axis=-1)


def _delta(k_chunk, kd, v_chunk):
    kt = (k_chunk.astype(F32) * kd).T.astype(BF16)
    return jnp.dot(kt, v_chunk, preferred_element_type=F32)


def _ret_state_kernel(dl_ref, kf_ref, vf_ref, kb_ref, vb_ref, fin_ref, bin_ref, sf, sb):
    head = pl.program_id(0)

    @pl.when(pl.program_id(1) == 0)
    def _():
        sf[...] = jnp.zeros_like(sf)
        sb[...] = jnp.zeros_like(sb)

    fin_ref[...] = sf[...]
    bin_ref[...] = sb[...]
    lg_f, lg_b = _decays(dl_ref, head)
    _, kd_f, kd_b, _, _, cd_f, cd_b = _chunk_tables(lg_f, lg_b)
    cd_f, cd_b = _wide(cd_f), _wide(cd_b)
    n_chunks = RET_BLOCK // RET_CHUNK

    def scan(t, carry):
        rf = pl.ds(pl.multiple_of(t * RET_CHUNK, RET_CHUNK), RET_CHUNK)
        rb = pl.ds(pl.multiple_of((n_chunks - 1 - t) * RET_CHUNK, RET_CHUNK), RET_CHUNK)
        sf[...] = sf[...] * cd_f + _delta(kf_ref[rf, :], kd_f, vf_ref[rf, :])
        sb[...] = sb[...] * cd_b + _delta(kb_ref[rb, :], kd_b, vb_ref[rb, :])
        return carry

    lax.fori_loop(0, n_chunks, scan, 0, unroll=2)


def _retention_states(p3, dl_tile):
    nb = p3.shape[0]
    k_cb = (CB_KR * COL) // R_KEY_DIM
    v_cb = OFF_VR // R_VAL_DIM
    shape = jax.ShapeDtypeStruct((nb, R_HEADS, R_KEY_DIM, R_VAL_DIM), F32)
    return pl.pallas_call(
        _ret_state_kernel,
        grid=(R_HEADS, nb),
        in_specs=[
            pl.BlockSpec((2 * R_HEADS, LANES), lambda h, j: (0, 0)),
            pl.BlockSpec((None, RET_BLOCK, R_KEY_DIM), lambda h, j: (j, 0, k_cb + h)),
            pl.BlockSpec((None, RET_BLOCK, R_VAL_DIM), lambda h, j: (j, 0, v_cb + h)),
            pl.BlockSpec((None, RET_BLOCK, R_KEY_DIM), lambda h, j: (nb - 1 - j, 0, k_cb + h)),
            pl.BlockSpec((None, RET_BLOCK, R_VAL_DIM), lambda h, j: (nb - 1 - j, 0, v_cb + h)),
        ],
        out_specs=[
            pl.BlockSpec((None, None, R_KEY_DIM, R_VAL_DIM), lambda h, j: (j, h, 0, 0)),
            pl.BlockSpec((None, None, R_KEY_DIM, R_VAL_DIM), lambda h, j: (nb - 1 - j, h, 0, 0)),
        ],
        out_shape=[shape, shape],
        scratch_shapes=[pltpu.VMEM((R_KEY_DIM, R_VAL_DIM), F32), pltpu.VMEM((R_KEY_DIM, R_VAL_DIM), F32)],
        compiler_params=_params(("parallel", "arbitrary")),
        name="retention_states",
    )(dl_tile, p3, p3, p3, p3)


def _ret_kernel(*refs, has_state):
    if has_state:
        dl_ref, gn_ref, q_ref, k_ref, v_ref, z_ref, fin_ref, bin_ref, o_ref, sf, sb, s_all = refs
    else:
        dl_ref, gn_ref, q_ref, k_ref, v_ref, z_ref, o_ref, sf, sb, s_all = refs
    head = pl.program_id(1)
    lg_f, lg_b = _decays(dl_ref, head)
    intra, kd_f, kd_b, qd_f, qd_b, cd_f, cd_b = _chunk_tables(lg_f, lg_b)
    cd_f, cd_b = _wide(cd_f), _wide(cd_b)
    gn = gn_ref[...]
    if has_state:
        sf[...] = fin_ref[...]
        sb[...] = bin_ref[...]
    else:
        sf[...] = jnp.zeros_like(sf)
        sb[...] = jnp.zeros_like(sb)
    n_chunks = RET_BLOCK // RET_CHUNK

    def scan(t, carry):
        tb = n_chunks - 1 - t
        rf = pl.ds(pl.multiple_of(t * RET_CHUNK, RET_CHUNK), RET_CHUNK)
        rb = pl.ds(pl.multiple_of(tb * RET_CHUNK, RET_CHUNK), RET_CHUNK)
        state_f = sf[...]
        s_all[t, 0:R_KEY_DIM, :] = state_f.astype(BF16)
        sf[...] = state_f * cd_f + _delta(k_ref[rf, :], kd_f, v_ref[rf, :])
        state_b = sb[...]
        s_all[tb, R_KEY_DIM:2 * R_KEY_DIM, :] = state_b.astype(BF16)
        sb[...] = state_b * cd_b + _delta(k_ref[rb, :], kd_b, v_ref[rb, :])
        return carry

    def emit(n, carry):
        rows = pl.ds(pl.multiple_of(n * RET_CHUNK, RET_CHUNK), RET_CHUNK)
        q, k, v = q_ref[rows, :], k_ref[rows, :], v_ref[rows, :]
        qf = q.astype(F32)
        qq = jnp.concatenate([(qf * qd_f).astype(BF16), (qf * qd_b).astype(BF16)], axis=-1)
        sc = lax.dot_general(q, k, (((1,), (1,)), ((), ())), preferred_element_type=F32) * intra
        y = (jnp.dot(sc.astype(BF16), v, preferred_element_type=F32)
             + jnp.dot(qq, s_all[n], preferred_element_type=F32))
        mu = jnp.mean(y, axis=-1, keepdims=True)
        yc = y - mu
        var = jnp.mean(yc * yc, axis=-1, keepdims=True)
        yn = yc * lax.rsqrt(var + EPS) * gn
        o_ref[rows, :] = (yn * _silu(z_ref[rows, :].astype(F32))).astype(BF16)
        return carry

    lax.fori_loop(0, n_chunks, scan, 0, unroll=4)
    lax.fori_loop(0, n_chunks, emit, 0, unroll=4)


def _retention(p3, dl_tile, gn, states):
    nb = p3.shape[0]
    q_cb = (CB_QR * COL) // R_KEY_DIM
    k_cb = (CB_KR * COL) // R_KEY_DIM
    v_cb = OFF_VR // R_VAL_DIM
    z_cb = OFF_ZR // R_VAL_DIM
    in_specs = [
        pl.BlockSpec((2 * R_HEADS, LANES), lambda b, h: (0, 0)),
        pl.BlockSpec((1, R_VAL_DIM), lambda b, h: (0, h)),
        pl.BlockSpec((None, RET_BLOCK, R_KEY_DIM), lambda b, h: (b, 0, q_cb + h)),
        pl.BlockSpec((None, RET_BLOCK, R_KEY_DIM), lambda b, h: (b, 0, k_cb + h)),
        pl.BlockSpec((None, RET_BLOCK, R_VAL_DIM), lambda b, h: (b, 0, v_cb + h)),
        pl.BlockSpec((None, RET_BLOCK, R_VAL_DIM), lambda b, h: (b, 0, z_cb + h)),
    ]
    args = [dl_tile, gn, p3, p3, p3, p3]
    if states is not None:
        st = pl.BlockSpec((None, None, R_KEY_DIM, R_VAL_DIM), lambda b, h: (b, h, 0, 0))
        in_specs += [st, st]
        args += list(states)
    n_chunks = RET_BLOCK // RET_CHUNK
    out = pl.pallas_call(
        functools.partial(_ret_kernel, has_state=states is not None),
        grid=(nb, R_HEADS),
        in_specs=in_specs,
        out_specs=pl.BlockSpec((None, RET_BLOCK, R_VAL_DIM), lambda b, h: (b, 0, h)),
        out_shape=jax.ShapeDtypeStruct((nb, RET_BLOCK, R_V_WIDTH), BF16),
        scratch_shapes=[
            pltpu.VMEM((R_KEY_DIM, R_VAL_DIM), F32),
            pltpu.VMEM((R_KEY_DIM, R_VAL_DIM), F32),
            pltpu.VMEM((n_chunks, 2 * R_KEY_DIM, R_VAL_DIM), BF16),
        ],
        compiler_params=_params(("parallel", "parallel")),
        name="retention",
    )(*args)
    return out.reshape(nb * RET_BLOCK, R_V_WIDTH)


def _out_kernel(x_ref, mod_ref, o1_ref, o4_ref, o16_ref, l1_ref, l4_ref, l16_ref, za_ref, ur_ref, ga_ref, gr_ref,
                wa_ref, wb_ref, wo_ref, y_ref, o_scr, l_scr):
    tm = x_ref.shape[0]
    for gi, (dil, o_ref, l_ref) in enumerate(((4, o4_ref, l4_ref), (16, o16_ref, l16_ref))):
        n = tm // dil
        for r in range(dil):
            l_scr[gi, pl.ds(r, n, stride=dil), :] = l_ref[r]
            for c in range(N_CHUNK):
                o_scr[gi, c, pl.ds(r, n, stride=dil), :] = o_ref[r, :, c * LANES:(c + 1) * LANES].astype(F32)

    lses = (l1_ref[...], l_scr[0], l_scr[1])
    top = jnp.maximum(jnp.maximum(lses[0], lses[1]), lses[2])
    es = [jnp.exp(v - top) for v in lses]
    den = es[0] + es[1] + es[2]
    ws = [e / den for e in es]
    lane = lax.broadcasted_iota(jnp.int32, (tm, LANES), 1)
    first_head = lane < A_HEAD_DIM
    ua = []
    for c in range(N_CHUNK):
        cols = slice(c * LANES, (c + 1) * LANES)
        outs = (o1_ref[:, cols].astype(F32), o_scr[0, c], o_scr[1, c])
        ya = jnp.zeros((tm, LANES), F32)
        for g in range(3):
            w_pair = jnp.where(first_head, ws[g][:, 2 * c:2 * c + 1], ws[g][:, 2 * c + 1:2 * c + 2])
            ya = ya + w_pair * outs[g]
        ua.append((ya * _silu(za_ref[:, cols].astype(F32))).astype(BF16))
    ua = jnp.concatenate(ua, axis=-1)
    ya_p = jnp.dot(ua, wa_ref[...], preferred_element_type=F32)
    yr_p = jnp.dot(ur_ref[...], wb_ref[...], preferred_element_type=F32)
    merged = jax.nn.sigmoid(ga_ref[...].astype(F32)) * ya_p + jax.nn.sigmoid(gr_ref[...].astype(F32)) * yr_p
    out = jnp.dot(merged.astype(BF16), wo_ref[...], preferred_element_type=F32)
    y_ref[...] = x_ref[...] + mod_ref[2:3, :] * out


def _output(x2, mod3, layer, b_off, seq, attn, p2, ur, wa, wb, wo):
    t = x2.shape[0]
    per_seq = seq // TM_OUT
    const = dict(pipeline_mode=pl.Buffered(1))
    (o1, l1), (o4, l4), (o16, l16) = attn

    def rows(width, cb=0):
        return pl.BlockSpec((TM_OUT, width), lambda i: (i, cb))

    def dilated(dil, width):
        return pl.BlockSpec((None, dil, TM_OUT // dil, width), lambda i: (i // per_seq, 0, i % per_seq, 0))

    return pl.pallas_call(
        _out_kernel,
        grid=(t // TM_OUT,),
        in_specs=[
            rows(D_MODEL),
            pl.BlockSpec((None, None, 3, D_MODEL), lambda i: (layer, i // per_seq + b_off, 0, 0)),
            rows(A_WIDTH), dilated(4, A_WIDTH), dilated(16, A_WIDTH),
            rows(LANES), dilated(4, LANES), dilated(16, LANES),
            rows(A_WIDTH, CB_ZA),
            rows(R_V_WIDTH),
            rows(D_MODEL, OFF_GA // D_MODEL),
            rows(D_MODEL, OFF_GR // D_MODEL),
            pl.BlockSpec((A_WIDTH, D_MODEL), lambda i: (0, 0), **const),
            pl.BlockSpec((R_V_WIDTH, D_MODEL), lambda i: (0, 0), **const),
            pl.BlockSpec((D_MODEL, D_MODEL), lambda i: (0, 0), **const),
        ],
        out_specs=rows(D_MODEL),
        out_shape=jax.ShapeDtypeStruct((t, D_MODEL), F32),
        scratch_shapes=[pltpu.VMEM((2, N_CHUNK, TM_OUT, LANES), F32), pltpu.VMEM((2, TM_OUT, LANES), F32)],
        compiler_params=_params(("parallel",)),
        name="merge_out_proj",
    )(x2, mod3, o1.reshape(t, A_WIDTH), o4, o16, l1.reshape(t, LANES), l4, l16, p2, ur, p2, p2, wa, wb, wo)


def _mixer_layer(x2, batch, seq, mod3, layer, b_off, lw, tables, e_bd):
    p2, a4, a16 = _inproj(x2, batch, seq, mod3, layer, b_off, lw["norm_g"], lw["w_in"], lw["gq"], lw["gk"], e_bd,
                          tables)
    attn = [_attention(qkv, dil) for qkv, dil in zip((p2.reshape(batch, 1, seq, IN_WIDTH), a4, a16), DILATIONS)]
    nb = batch * seq // RET_BLOCK
    pr = p2.reshape(nb, RET_BLOCK, IN_WIDTH)
    states = _retention_states(pr, lw["dl"]) if seq > RET_BLOCK else None
    ur = _retention(pr, lw["dl"], lw["gn"], states)
    return _output(x2, mod3, layer, b_off, seq, attn, p2, ur, lw["wa"], lw["wb"], lw["wo"])


def _layer_weights(layer, norm_g, w_in, q_norm_g, k_norm_g, ret_decay_logit, ret_norm_g, w_proj_a, w_proj_b, w_out):
    return {
        "norm_g": norm_g[layer].reshape(1, D_MODEL),
        "w_in": w_in[layer].astype(BF16),
        "gq": (jnp.tile(q_norm_g[layer], A_HEADS) * (A_HEAD_DIM ** -0.5 * LOG2E)).reshape(1, A_WIDTH),
        "gk": jnp.tile(k_norm_g[layer], A_HEADS).reshape(1, A_WIDTH),
        "dl": jnp.broadcast_to(ret_decay_logit[layer].astype(F32).reshape(2 * R_HEADS, 1), (2 * R_HEADS, LANES)),
        "gn": ret_norm_g[layer].reshape(1, R_V_WIDTH),
        "wa": w_proj_a[layer].astype(BF16),
        "wb": w_proj_b[layer].astype(BF16),
        "wo": w_out[layer].astype(BF16),
    }


def kernel(x_prompt, x_sample, c_prompt, c_sample, norm_g, w_ada, b_ada, w_in, q_norm_g, k_norm_g, ret_decay_logit,
           ret_norm_g, w_proj_a, w_proj_b, w_out):
    bp, sp, _ = x_prompt.shape
    bs, ss, _ = x_sample.shape
    pad = (-(bp + bs)) % 8
    c_all = jnp.concatenate([c_prompt, c_sample, jnp.zeros((pad, D_MODEL), F32)], axis=0)
    mod3 = _modulation(c_all, w_ada, b_ada).reshape(DEPTH, bp + bs + pad, 3, D_MODEL)
    head_of_lane = jnp.arange(A_WIDTH) // A_HEAD_DIM
    e_bd = (head_of_lane[:, None] == head_of_lane[None, :]).astype(BF16)
    tab_p = _rope_tables(sp)
    tab_s = tab_p if ss == sp else _rope_tables(ss)
    yp = x_prompt.reshape(bp * sp, D_MODEL)
    ys = x_sample.reshape(bs * ss, D_MODEL)
    for layer in range(DEPTH):
        lw = _layer_weights(layer, norm_g, w_in, q_norm_g, k_norm_g, ret_decay_logit, ret_norm_g, w_proj_a, w_proj_b,
                            w_out)
        yp = _mixer_layer(yp, bp, sp, mod3, layer, 0, lw, tab_p, e_bd)
        ys = _mixer_layer(ys, bs, ss, mod3, layer, bp, lw, tab_s, e_bd)
    return yp.reshape(bp, sp, D_MODEL), ys.reshape(bs, ss, D_MODEL)
```

```python
import functools
import math

import jax
import jax.numpy as jnp
from jax import lax
from jax.experimental import pallas as pl
from jax.experimental.pallas import tpu as pltpu

D_MODEL = 1024
DEPTH = 4
A_HEADS = 8
A_HEAD_DIM = 64
A_WIDTH = A_HEADS * A_HEAD_DIM
A_ROT_HALF = A_HEAD_DIM // 8
ATT_THETA = 500000.0
DILATIONS = (1, 4, 16)
HALF_WIN = 64
R_HEADS = 4
R_KEY_DIM = 128
R_VAL_DIM = 256
R_V_WIDTH = R_HEADS * R_VAL_DIM
RET_THETA = 10000.0
RET_CHUNK = 128
IN_WIDTH = 4 * A_WIDTH + 2 * R_HEADS * R_KEY_DIM + 2 * R_V_WIDTH + 2 * D_MODEL
EPS = 1e-6
NEG = -1e30
LOG2E = 1.4426950408889634

LANES = 128
COL = 512
N_COL = IN_WIDTH // COL
N_CHUNK = COL // LANES
E_BLK = 256
CB_QA, CB_KA, CB_VA, CB_ZA, CB_QR, CB_KR = 0, 1, 2, 3, 4, 5
OFF_VR, OFF_ZR, OFF_GA, OFF_GR = 3072, 4096, 5120, 6144
QKV_W = 3 * A_WIDTH
ROPE_TABLE_W = 5 * LANES

TM_IN = 512
TM_OUT = 512
OUT_SUB = 128
TQ_ATT = 512
Q_SUB = 128
K_WIN = Q_SUB + 2 * HALF_WIN
RET_BLOCK = 2048
VMEM_LIMIT = 56 * 1024 * 1024

F32 = jnp.float32
BF16 = jnp.bfloat16


def _silu(v):
    return v * jax.nn.sigmoid(v)


def _params(sem, vmem=VMEM_LIMIT):
    return pltpu.CompilerParams(dimension_semantics=sem, vmem_limit_bytes=vmem)


def _mod_kernel(c_ref, w_ref, b_ref, o_ref):
    s = _silu(c_ref[...])
    o_ref[...] = jnp.dot(s, w_ref[...], preferred_element_type=F32, precision=lax.Precision.HIGHEST) + b_ref[...]


def _modulation(c_all, w_ada, b_ada):
    nb = c_all.shape[0]
    return pl.pallas_call(
        _mod_kernel,
        grid=(DEPTH, 3),
        in_specs=[
            pl.BlockSpec((nb, D_MODEL), lambda l, j: (0, 0)),
            pl.BlockSpec((None, D_MODEL, D_MODEL), lambda l, j: (l, 0, j)),
            pl.BlockSpec((None, 1, D_MODEL), lambda l, j: (l, 0, j)),
        ],
        out_specs=pl.BlockSpec((None, nb, D_MODEL), lambda l, j: (l, 0, j)),
        out_shape=jax.ShapeDtypeStruct((DEPTH, nb, 3 * D_MODEL), F32),
        compiler_params=_params(("parallel", "parallel")),
        name="adaln_mod",
    )(c_all, w_ada, b_ada.reshape(DEPTH, 1, 3 * D_MODEL))


def _rope_tables(seq):
    pos = jnp.arange(seq, dtype=F32)
    fa = jnp.exp(-math.log(ATT_THETA) * jnp.arange(A_ROT_HALF, dtype=F32) / A_ROT_HALF)
    ang = pos[:, None] * fa[None, :]
    cos, sin = jnp.cos(ang), jnp.sin(ang)
    rest = A_HEAD_DIM - 2 * A_ROT_HALF
    one = jnp.ones((seq, rest), F32)
    zero = jnp.zeros((seq, rest), F32)
    z8 = jnp.zeros((seq, A_ROT_HALF), F32)
    att_c = jnp.tile(jnp.concatenate([cos, cos, one], 1), (1, 2))
    att_a = jnp.tile(jnp.concatenate([-sin, z8, zero], 1), (1, 2))
    att_b = jnp.tile(jnp.concatenate([z8, sin, zero], 1), (1, 2))
    half = R_KEY_DIM // 2
    fr = jnp.exp(-math.log(RET_THETA) * jnp.arange(half, dtype=F32) / half)
    ang = pos[:, None] * fr[None, :]
    cos, sin = jnp.cos(ang), jnp.sin(ang)
    ret_c = jnp.concatenate([cos, cos], 1)
    ret_s = jnp.concatenate([-sin, sin], 1)
    return jnp.concatenate([att_c, att_a, att_b, ret_c, ret_s], 1)


def _inproj_kernel(x_ref, mod_ref, g_ref, w_ref, gq_ref, gk_ref, e_ref, tab_ref, o_ref, a4_ref, a16_ref, rows_scr):
    x = x_ref[...]
    ms = jnp.mean(x * x, axis=-1, keepdims=True)
    y = x * lax.rsqrt(ms + EPS) * g_ref[...]
    h = (y * (1.0 + mod_ref[1:2, :]) + mod_ref[0:1, :]).astype(BF16)

    att_c = tab_ref[:, 0 * LANES:1 * LANES]
    att_a = tab_ref[:, 1 * LANES:2 * LANES]
    att_b = tab_ref[:, 2 * LANES:3 * LANES]
    ret_c = tab_ref[:, 3 * LANES:4 * LANES]
    ret_s = tab_ref[:, 4 * LANES:5 * LANES]

    def emit_qkv(j, c, val):
        lo = j * COL + c * LANES
        o_ref[:, lo:lo + LANES] = val.astype(BF16)
        rows_scr[0, c] = val
        n4 = TM_IN // 4
        n16 = TM_IN // 16
        for r4 in range(4):
            by4 = rows_scr[0, c, pl.ds(r4, n4, stride=4), :]
            a4_ref[r4, :, lo:lo + LANES] = by4.astype(BF16)
            rows_scr[1, c, r4 * n4:(r4 + 1) * n4, :] = by4
        for r4 in range(4):
            for r in range(4):
                by16 = rows_scr[1, c, pl.ds(r4 * n4 + r, n16, stride=4), :]
                a16_ref[4 * r + r4, :, lo:lo + LANES] = by16.astype(BF16)

    for j in range(N_COL):
        acc = jnp.dot(h, w_ref[:, j * COL:(j + 1) * COL], preferred_element_type=F32)
        if j in (CB_QA, CB_KA):
            sq = (acc * acc).astype(BF16)
            ss = jnp.concatenate(
                [jnp.dot(sq[:, s * E_BLK:(s + 1) * E_BLK], e_ref[...], preferred_element_type=F32)
                 for s in range(COL // E_BLK)], axis=-1)
            gain = gq_ref[...] if j == CB_QA else gk_ref[...]
            yn = acc * lax.rsqrt(ss * (1.0 / A_HEAD_DIM) + EPS) * gain
            for c in range(N_CHUNK):
                xc = yn[:, c * LANES:(c + 1) * LANES]
                rot = (xc * att_c + pltpu.roll(xc, LANES - A_ROT_HALF, 1) * att_a
                       + pltpu.roll(xc, A_ROT_HALF, 1) * att_b)
                emit_qkv(j, c, rot)
        elif j == CB_VA:
            for c in range(N_CHUNK):
                emit_qkv(j, c, acc[:, c * LANES:(c + 1) * LANES])
        elif j in (CB_QR, CB_KR):
            for c in range(N_CHUNK):
                xc = acc[:, c * LANES:(c + 1) * LANES]
                rot = xc * ret_c + pltpu.roll(xc, R_KEY_DIM // 2, 1) * ret_s
                if j == CB_KR:
                    rot = rot * (R_KEY_DIM ** -0.5)
                o_ref[:, j * COL + c * LANES:j * COL + (c + 1) * LANES] = rot.astype(BF16)
        else:
            o_ref[:, j * COL:(j + 1) * COL] = acc.astype(BF16)


def _inproj(x2, batch, seq, mod3, layer, b_off, norm_g, w_in, gq, gk, e_bd, tables):
    t = x2.shape[0]
    per_seq = seq // TM_IN
    const = dict(pipeline_mode=pl.Buffered(1))

    def dilated(dil):
        return pl.BlockSpec((None, dil, TM_IN // dil, QKV_W), lambda i: (i // per_seq, 0, i % per_seq, 0))

    return pl.pallas_call(
        _inproj_kernel,
        grid=(t // TM_IN,),
        in_specs=[
            pl.BlockSpec((TM_IN, D_MODEL), lambda i: (i, 0)),
            pl.BlockSpec((None, None, 3, D_MODEL), lambda i: (layer, i // per_seq + b_off, 0, 0)),
            pl.BlockSpec((1, D_MODEL), lambda i: (0, 0)),
            pl.BlockSpec((D_MODEL, IN_WIDTH), lambda i: (0, 0), **const),
            pl.BlockSpec((1, COL), lambda i: (0, 0)),
            pl.BlockSpec((1, COL), lambda i: (0, 0)),
            pl.BlockSpec((E_BLK, E_BLK), lambda i: (0, 0), **const),
            pl.BlockSpec((TM_IN, ROPE_TABLE_W), lambda i: (i % per_seq, 0)),
        ],
        out_specs=[pl.BlockSpec((TM_IN, IN_WIDTH), lambda i: (i, 0)), dilated(4), dilated(16)],
        out_shape=[
            jax.ShapeDtypeStruct((t, IN_WIDTH), BF16),
            jax.ShapeDtypeStruct((batch, 4, seq // 4, QKV_W), BF16),
            jax.ShapeDtypeStruct((batch, 16, seq // 16, QKV_W), BF16),
        ],
        scratch_shapes=[pltpu.VMEM((2, N_CHUNK, TM_IN, LANES), F32)],
        compiler_params=_params(("parallel",)),
        name="in_proj",
    )(x2, mod3, norm_g, w_in, gq, gk, e_bd, tables)


def _attn_kernel(q_ref, kp_ref, kc_ref, kn_ref, vp_ref, vc_ref, vn_ref, o_ref, st_ref, kbuf, vbuf, *, tq, nr, length):
    kbuf[:, 0:HALF_WIN, :] = kp_ref[...]
    kbuf[:, HALF_WIN:HALF_WIN + tq, :] = kc_ref[...]
    kbuf[:, HALF_WIN + tq:, :] = kn_ref[...]
    key_lane = lax.broadcasted_iota(jnp.int32, (nr, tq + 2 * HALF_WIN, LANES), 2)
    for p in range(A_WIDTH // LANES):
        src = slice(p * LANES, (p + 1) * LANES)
        dst = slice(2 * p * LANES, (2 * p + 1) * LANES)
        vbuf[:, 0:HALF_WIN, dst] = vp_ref[:, :, src]
        vbuf[:, HALF_WIN:HALF_WIN + tq, dst] = vc_ref[:, :, src]
        vbuf[:, HALF_WIN + tq:, dst] = vn_ref[:, :, src]
        den_lanes = (key_lane == A_HEADS + 2 * p) | (key_lane == A_HEADS + 2 * p + 1)
        vbuf[:, :, (2 * p + 1) * LANES:(2 * p + 2) * LANES] = jnp.where(den_lanes, 1.0, 0.0).astype(BF16)
    q0 = pl.program_id(2) * tq

    kj = lax.broadcasted_iota(jnp.int32, (Q_SUB, K_WIN), 1)
    qi = lax.broadcasted_iota(jnp.int32, (Q_SUB, K_WIN), 0)
    band = (kj >= qi) & (kj <= qi + 2 * HALF_WIN)
    lane = lax.broadcasted_iota(jnp.int32, (Q_SUB, LANES), 1)
    first_head = lane < A_HEAD_DIM
    odd_lane = (lane & 1) == 1
    sub_per_res = tq // Q_SUB

    def sub_block(t, carry):
        ri = t // sub_per_res
        r0 = pl.multiple_of((t % sub_per_res) * Q_SUB, Q_SUB)
        kpos = kj + (q0 + r0 - HALF_WIN)
        bias = jnp.where(band & (kpos >= 0) & (kpos < length), 0.0, NEG)
        bias = jnp.concatenate([bias, bias], axis=0)
        stat = jnp.zeros((Q_SUB, LANES), F32)
        for p in range(A_WIDTH // LANES):
            cols = slice(p * LANES, (p + 1) * LANES)
            q2 = q_ref[ri, pl.ds(r0, Q_SUB), cols]
            k2 = kbuf[ri, pl.ds(r0, K_WIN), cols]
            v2 = vbuf[ri, pl.ds(r0, K_WIN), 2 * p * LANES:(2 * p + 2) * LANES]
            zero = jnp.zeros_like(q2)
            qs = jnp.concatenate([jnp.where(first_head, q2, zero), jnp.where(first_head, zero, q2)], axis=0)
            sc = lax.dot_general(qs, k2, (((1,), (1,)), ((), ())), preferred_element_type=F32) + bias
            m = jnp.max(sc, axis=-1, keepdims=True)
            pr = jnp.exp2(sc - m)
            pvd = jnp.dot(pr.astype(BF16), v2, preferred_element_type=F32)
            pv, den = pvd[:, :LANES], pvd[:, LANES:]
            stat = stat + jnp.where(odd_lane, den[Q_SUB:], den[:Q_SUB])
            stat = jnp.where(lane == 2 * p, m[:Q_SUB], stat)
            stat = jnp.where(lane == 2 * p + 1, m[Q_SUB:], stat)
            o_ref[ri, pl.ds(r0, Q_SUB), cols] = jnp.where(first_head, pv[:Q_SUB], pv[Q_SUB:]).astype(BF16)
        st_ref[ri, pl.ds(r0, Q_SUB), :] = stat
        return carry

    lax.fori_loop(0, nr * sub_per_res, sub_block, 0)


def _attention(qkv, dil):
    b, _, length, _ = qkv.shape
    tq = min(TQ_ATT, length)
    nr = min(dil, TQ_ATT // tq)
    nq = length // tq
    halo_per_q = tq // HALF_WIN
    last_halo = length // HALF_WIN - 1

    def cur(cb):
        return pl.BlockSpec((None, nr, tq, COL), lambda bi, r, i: (bi, r, i, cb))

    def prev(cb):
        return pl.BlockSpec((None, nr, HALF_WIN, COL),
                            lambda bi, r, i: (bi, r, jnp.maximum(i * halo_per_q - 1, 0), cb))

    def nxt(cb):
        return pl.BlockSpec((None, nr, HALF_WIN, COL),
                            lambda bi, r, i: (bi, r, jnp.minimum((i + 1) * halo_per_q, last_halo), cb))

    return pl.pallas_call(
        functools.partial(_attn_kernel, tq=tq, nr=nr, length=length),
        grid=(b, dil // nr, nq),
        in_specs=[cur(CB_QA), prev(CB_KA), cur(CB_KA), nxt(CB_KA), prev(CB_VA), cur(CB_VA), nxt(CB_VA)],
        out_specs=[
            pl.BlockSpec((None, nr, tq, A_WIDTH), lambda bi, r, i: (bi, r, i, 0)),
            pl.BlockSpec((None, nr, tq, LANES), lambda bi, r, i: (bi, r, i, 0)),
        ],
        out_shape=[
            jax.ShapeDtypeStruct((b, dil, length, A_WIDTH), BF16),
            jax.ShapeDtypeStruct((b, dil, length, LANES), F32),
        ],
        scratch_shapes=[pltpu.VMEM((nr, tq + 2 * HALF_WIN, COL), BF16),
                        pltpu.VMEM((nr, tq + 2 * HALF_WIN, 2 * COL), BF16)],
        compiler_params=_params(("parallel", "parallel", "parallel")),
        name=f"window_attn_d{dil}",
    )(qkv, qkv, qkv, qkv, qkv, qkv, qkv)


def _log_sigmoid(v):
    return jnp.minimum(v, 0.0) - jnp.log1p(jnp.exp(-jnp.abs(v)))


def _decays(dl_ref, head):
    lg_f = _log_sigmoid(dl_ref[pl.ds(head, 1), :])
    lg_b = _log_sigmoid(dl_ref[pl.ds(head + R_HEADS, 1), :])
    return lg_f, lg_b


def _chunk_tables(lg_f, lg_b):
    c = RET_CHUNK
    row = lax.broadcasted_iota(jnp.int32, (c, c), 0).astype(F32)
    col = lax.broadcasted_iota(jnp.int32, (c, c), 1).astype(F32)
    rel = row - col
    intra = (jnp.where(rel >= 0, jnp.exp(jnp.maximum(rel, 0.0) * lg_f), 0.0)
             + jnp.where(rel <= 0, jnp.exp(jnp.maximum(-rel, 0.0) * lg_b), 0.0))
    kd_f = jnp.exp((c - 1 - row) * lg_f)
    kd_b = jnp.exp(row * lg_b)
    qd_f = jnp.exp((row + 1.0) * lg_f)
    qd_b = jnp.exp((c - row) * lg_b)
    cd_f = jnp.exp(c * lg_f)
    cd_b = jnp.exp(c * lg_b)
    return intra, kd_f, kd_b, qd_f, qd_b, cd_f, cd_b


def _wide(v):
    return jnp.concatenate([v, v], axis=-1)


def _delta(k_chunk, kd, v_chunk):
    kt = (k_chunk.astype(F32) * kd).T.astype(BF16)
    return jnp.dot(kt, v_chunk, preferred_element_type=F32)


def _ret_state_kernel(dl_ref, kf_ref, vf_ref, kb_ref, vb_ref, fin_ref, bin_ref, sf, sb):
    head = pl.program_id(0)

    @pl.when(pl.program_id(1) == 0)
    def _():
        sf[...] = jnp.zeros_like(sf)
        sb[...] = jnp.zeros_like(sb)

    fin_ref[...] = sf[...]
    bin_ref[...] = sb[...]
    lg_f, lg_b = _decays(dl_ref, head)
    _, kd_f, kd_b, _, _, cd_f, cd_b = _chunk_tables(lg_f, lg_b)
    cd_f, cd_b = _wide(cd_f), _wide(cd_b)
    n_chunks = RET_BLOCK // RET_CHUNK

    def scan(t, carry):
        rf = pl.ds(pl.multiple_of(t * RET_CHUNK, RET_CHUNK), RET_CHUNK)
        rb = pl.ds(pl.multiple_of((n_chunks - 1 - t) * RET_CHUNK, RET_CHUNK), RET_CHUNK)
        sf[...] = sf[...] * cd_f + _delta(kf_ref[rf, :], kd_f, vf_ref[rf, :])
        sb[...] = sb[...] * cd_b + _delta(kb_ref[rb, :], kd_b, vb_ref[rb, :])
        return carry

    lax.fori_loop(0, n_chunks, scan, 0, unroll=2)


def _retention_states(p3, dl_tile):
    nb = p3.shape[0]
    k_cb = (CB_KR * COL) // R_KEY_DIM
    v_cb = OFF_VR // R_VAL_DIM
    shape = jax.ShapeDtypeStruct((nb, R_HEADS, R_KEY_DIM, R_VAL_DIM), F32)
    return pl.pallas_call(
        _ret_state_kernel,
        grid=(R_HEADS, nb),
        in_specs=[
            pl.BlockSpec((2 * R_HEADS, LANES), lambda h, j: (0, 0)),
            pl.BlockSpec((None, RET_BLOCK, R_KEY_DIM), lambda h, j: (j, 0, k_cb + h)),
            pl.BlockSpec((None, RET_BLOCK, R_VAL_DIM), lambda h, j: (j, 0, v_cb + h)),
            pl.BlockSpec((None, RET_BLOCK, R_KEY_DIM), lambda h, j: (nb - 1 - j, 0, k_cb + h)),
            pl.BlockSpec((None, RET_BLOCK, R_VAL_DIM), lambda h, j: (nb - 1 - j, 0, v_cb + h)),
        ],
        out_specs=[
            pl.BlockSpec((None, None, R_KEY_DIM, R_VAL_DIM), lambda h, j: (j, h, 0, 0)),
            pl.BlockSpec((None, None, R_KEY_DIM, R_VAL_DIM), lambda h, j: (nb - 1 - j, h, 0, 0)),
        ],
        out_shape=[shape, shape],
        scratch_shapes=[pltpu.VMEM((R_KEY_DIM, R_VAL_DIM), F32), pltpu.VMEM((R_KEY_DIM, R_VAL_DIM), F32)],
        compiler_params=_params(("parallel", "arbitrary")),
        name="retention_states",
    )(dl_tile, p3, p3, p3, p3)


def _ret_kernel(*refs, has_state):
    if has_state:
        dl_ref, gn_ref, q_ref, k_ref, v_ref, z_ref, fin_ref, bin_ref, o_ref, sf, sb, s_all = refs
    else:
        dl_ref, gn_ref, q_ref, k_ref, v_ref, z_ref, o_ref, sf, sb, s_all = refs
    head = pl.program_id(1)
    lg_f, lg_b = _decays(dl_ref, head)
    intra, kd_f, kd_b, qd_f, qd_b, cd_f, cd_b = _chunk_tables(lg_f, lg_b)
    cd_f, cd_b = _wide(cd_f), _wide(cd_b)
    gn = gn_ref[...]
    if has_state:
        sf[...] = fin_ref[...]
        sb[...] = bin_ref[...]
    else:
        sf[...] = jnp.zeros_like(sf)
        sb[...] = jnp.zeros_like(sb)
    n_chunks = RET_BLOCK // RET_CHUNK

    def scan(t, carry):
        tb = n_chunks - 1 - t
        rf = pl.ds(pl.multiple_of(t * RET_CHUNK, RET_CHUNK), RET_CHUNK)
        rb = pl.ds(pl.multiple_of(tb * RET_CHUNK, RET_CHUNK), RET_CHUNK)
        state_f = sf[...]
        s_all[t, 0:R_KEY_DIM, :] = state_f.astype(BF16)
        sf[...] = state_f * cd_f + _delta(k_ref[rf, :], kd_f, v_ref[rf, :])
        state_b = sb[...]
        s_all[tb, R_KEY_DIM:2 * R_KEY_DIM, :] = state_b.astype(BF16)
        sb[...] = state_b * cd_b + _delta(k_ref[rb, :], kd_b, v_ref[rb, :])
        return carry

    def emit(n, carry):
        rows = pl.ds(pl.multiple_of(n * RET_CHUNK, RET_CHUNK), RET_CHUNK)
        q, k, v = q_ref[rows, :], k_ref[rows, :], v_ref[rows, :]
        qf = q.astype(F32)
        qq = jnp.concatenate([(qf * qd_f).astype(BF16), (qf * qd_b).astype(BF16)], axis=-1)
        sc = lax.dot_general(q, k, (((1,), (1,)), ((), ())), preferred_element_type=F32) * intra
        y = (jnp.dot(sc.astype(BF16), v, preferred_element_type=F32)
             + jnp.dot(qq, s_all[n], preferred_element_type=F32))
        mu = jnp.mean(y, axis=-1, keepdims=True)
        yc = y - mu
        var = jnp.mean(yc * yc, axis=-1, keepdims=True)
        yn = yc * lax.rsqrt(var + EPS) * gn
        o_ref[rows, :] = (yn * _silu(z_ref[rows, :].astype(F32))).astype(BF16)
        return carry

    lax.fori_loop(0, n_chunks, scan, 0, unroll=4)
    lax.fori_loop(0, n_chunks, emit, 0, unroll=4)


def _retention(p3, dl_tile, gn, states):
    nb = p3.shape[0]
    q_cb = (CB_QR * COL) // R_KEY_DIM
    k_cb = (CB_KR * COL) // R_KEY_DIM
    v_cb = OFF_VR // R_VAL_DIM
    z_cb = OFF_ZR // R_VAL_DIM
    in_specs = [
        pl.BlockSpec((2 * R_HEADS, LANES), lambda b, h: (0, 0)),
        pl.BlockSpec((1, R_VAL_DIM), lambda b, h: (0, h)),
        pl.BlockSpec((None, RET_BLOCK, R_KEY_DIM), lambda b, h: (b, 0, q_cb + h)),
        pl.BlockSpec((None, RET_BLOCK, R_KEY_DIM), lambda b, h: (b, 0, k_cb + h)),
        pl.BlockSpec((None, RET_BLOCK, R_VAL_DIM), lambda b, h: (b, 0, v_cb + h)),
        pl.BlockSpec((None, RET_BLOCK, R_VAL_DIM), lambda b, h: (b, 0, z_cb + h)),
    ]
    args = [dl_tile, gn, p3, p3, p3, p3]
    if states is not None:
        st = pl.BlockSpec((None, None, R_KEY_DIM, R_VAL_DIM), lambda b, h: (b, h, 0, 0))
        in_specs += [st, st]
        args += list(states)
    n_chunks = RET_BLOCK // RET_CHUNK
    out = pl.pallas_call(
        functools.partial(_ret_kernel, has_state=states is not None),
        grid=(nb, R_HEADS),
        in_specs=in_specs,
        out_specs=pl.BlockSpec((None, RET_BLOCK, R_VAL_DIM), lambda b, h: (b, 0, h)),
        out_shape=jax.ShapeDtypeStruct((nb, RET_BLOCK, R_V_WIDTH), BF16),
        scratch_shapes=[
            pltpu.VMEM((R_KEY_DIM, R_VAL_DIM), F32),
            pltpu.VMEM((R_KEY_DIM, R_VAL_DIM), F32),
            pltpu.VMEM((n_chunks, 2 * R_KEY_DIM, R_VAL_DIM), BF16),
        ],
        compiler_params=_params(("parallel", "parallel")),
        name="retention",
    )(*args)
    return out.reshape(nb * RET_BLOCK, R_V_WIDTH)


def _out_kernel(x_ref, mod_ref, o1_ref, o4_ref, o16_ref, l1_ref, l4_ref, l16_ref, za_ref, ur_ref, ga_ref, gr_ref,
                wa_ref, wb_ref, wo_ref, y_ref, o_scr, l_scr):
    tm = x_ref.shape[0]
    for gi, (dil, o_ref, l_ref) in enumerate(((4, o4_ref, l4_ref), (16, o16_ref, l16_ref))):
        n = tm // dil
        for r in range(dil):
            l_scr[gi, pl.ds(r, n, stride=dil), :] = l_ref[r]
            for c in range(N_CHUNK):
                o_scr[gi, c, pl.ds(r, n, stride=dil), :] = o_ref[r, :, c * LANES:(c + 1) * LANES].astype(F32)

    lane = lax.broadcasted_iota(jnp.int32, (OUT_SUB, LANES), 1)
    first_head = lane < A_HEAD_DIM
    for s in range(tm // OUT_SUB):
        rows = slice(s * OUT_SUB, (s + 1) * OUT_SUB)
        stats = (l1_ref[rows, :], l_scr[0, rows, :], l_scr[1, rows, :])
        top = jnp.maximum(jnp.maximum(stats[0], stats[1]), stats[2])
        es = [jnp.exp2(v - top) for v in stats]
        dens = [pltpu.roll(v, LANES - A_HEADS, 1) for v in stats]
        total = es[0] * dens[0] + es[1] * dens[1] + es[2] * dens[2]
        total = jnp.where(lane < A_HEADS, total, 1.0)
        ws = [e / total for e in es]
        ua = []
        for c in range(N_CHUNK):
            cols = slice(c * LANES, (c + 1) * LANES)
            outs = (o1_ref[rows, cols].astype(F32), o_scr[0, c, rows, :], o_scr[1, c, rows, :])
            ya = jnp.zeros((OUT_SUB, LANES), F32)
            for g in range(3):
                w_pair = jnp.where(first_head, ws[g][:, 2 * c:2 * c + 1], ws[g][:, 2 * c + 1:2 * c + 2])
                ya = ya + w_pair * outs[g]
            ua.append((ya * _silu(za_ref[rows, cols].astype(F32))).astype(BF16))
        ua = jnp.concatenate(ua, axis=-1)
        ya_p = jnp.dot(ua, wa_ref[...], preferred_element_type=F32)
        yr_p = jnp.dot(ur_ref[rows, :], wb_ref[...], preferred_element_type=F32)
        merged = (jax.nn.sigmoid(ga_ref[rows, :].astype(F32)) * ya_p
                  + jax.nn.sigmoid(gr_ref[rows, :].astype(F32)) * yr_p)
        out = jnp.dot(merged.astype(BF16), wo_ref[...], preferred_element_type=F32)
        y_ref[rows, :] = x_ref[rows, :] + mod_ref[2:3, :] * out


def _output(x2, mod3, layer, b_off, seq, attn, p2, ur, wa, wb, wo):
    t = x2.shape[0]
    per_seq = seq // TM_OUT
    const = dict(pipeline_mode=pl.Buffered(1))
    (o1, l1), (o4, l4), (o16, l16) = attn

    def rows(width, cb=0):
        return pl.BlockSpec((TM_OUT, width), lambda i: (i, cb))

    def dilated(dil, width):
        return pl.BlockSpec((None, dil, TM_OUT // dil, width), lambda i: (i // per_seq, 0, i % per_seq, 0))

    return pl.pallas_call(
        _out_kernel,
        grid=(t // TM_OUT,),
        in_specs=[
            rows(D_MODEL),
            pl.BlockSpec((None, None, 3, D_MODEL), lambda i: (layer, i // per_seq + b_off, 0, 0)),
            rows(A_WIDTH), dilated(4, A_WIDTH), dilated(16, A_WIDTH),
            rows(LANES), dilated(4, LANES), dilated(16, LANES),
            rows(A_WIDTH, CB_ZA),
            rows(R_V_WIDTH),
            rows(D_MODEL, OFF_GA // D_MODEL),
            rows(D_MODEL, OFF_GR // D_MODEL),
            pl.BlockSpec((A_WIDTH, D_MODEL), lambda i: (0, 0), **const),
            pl.BlockSpec((R_V_WIDTH, D_MODEL), lambda i: (0, 0), **const),
            pl.BlockSpec((D_MODEL, D_MODEL), lambda i: (0, 0), **const),
        ],
        out_specs=rows(D_MODEL),
        out_shape=jax.ShapeDtypeStruct((t, D_MODEL), F32),
        scratch_shapes=[pltpu.VMEM((2, N_CHUNK, TM_OUT, LANES), F32), pltpu.VMEM((2, TM_OUT, LANES), F32)],
        compiler_params=_params(("parallel",)),
        name="merge_out_proj",
    )(x2, mod3, o1.reshape(t, A_WIDTH), o4, o16, l1.reshape(t, LANES), l4, l16, p2, ur, p2, p2, wa, wb, wo)


def _mixer_layer(x2, batch, seq, mod3, layer, b_off, lw, tables, e_bd):
    p2, a4, a16 = _inproj(x2, batch, seq, mod3, layer, b_off, lw["norm_g"], lw["w_in"], lw["gq"], lw["gk"], e_bd,
                          tables)
    attn = [_attention(qkv, dil) for qkv, dil in zip((p2.reshape(batch, 1, seq, IN_WIDTH), a4, a16), DILATIONS)]
    nb = batch * seq // RET_BLOCK
    pr = p2.reshape(nb, RET_BLOCK, IN_WIDTH)
    states = _retention_states(pr, lw["dl"]) if seq > RET_BLOCK else None
    ur = _retention(pr, lw["dl"], lw["gn"], states)
    return _output(x2, mod3, layer, b_off, seq, attn, p2, ur, lw["wa"], lw["wb"], lw["wo"])


def _layer_weights(layer, norm_g, w_in, q_norm_g, k_norm_g, ret_decay_logit, ret_norm_g, w_proj_a, w_proj_b, w_out):
    return {
        "norm_g": norm_g[layer].reshape(1, D_MODEL),
        "w_in": w_in[layer].astype(BF16),
        "gq": (jnp.tile(q_norm_g[layer], A_HEADS) * (A_HEAD_DIM ** -0.5 * LOG2E)).reshape(1, A_WIDTH),
        "gk": jnp.tile(k_norm_g[layer], A_HEADS).reshape(1, A_WIDTH),
        "dl": jnp.broadcast_to(ret_decay_logit[layer].astype(F32).reshape(2 * R_HEADS, 1), (2 * R_HEADS, LANES)),
        "gn": ret_norm_g[layer].reshape(1, R_V_WIDTH),
        "wa": w_proj_a[layer].astype(BF16),
        "wb": w_proj_b[layer].astype(BF16),
        "wo": w_out[layer].astype(BF16),
    }


def kernel(x_prompt, x_sample, c_prompt, c_sample, norm_g, w_ada, b_ada, w_in, q_norm_g, k_norm_g, ret_decay_logit,
           ret_norm_g, w_proj_a, w_proj_b, w_out):
    bp, sp, _ = x_prompt.shape
    bs, ss, _ = x_sample.shape
    pad = (-(bp + bs)) % 8
    c_all = jnp.concatenate([c_prompt, c_sample, jnp.zeros((pad, D_MODEL), F32)], axis=0)
    mod3 = _modulation(c_all, w_ada, b_ada).reshape(DEPTH, bp + bs + pad, 3, D_MODEL)
    head_of_lane = jnp.arange(E_BLK) // A_HEAD_DIM
    e_bd = (head_of_lane[:, None] == head_of_lane[None, :]).astype(BF16)
    tab_p = _rope_tables(sp)
    tab_s = tab_p if ss == sp else _rope_tables(ss)
    yp = x_prompt.reshape(bp * sp, D_MODEL)
    ys = x_sample.reshape(bs * ss, D_MODEL)
    for layer in range(DEPTH):
        lw = _layer_weights(layer, norm_g, w_in, q_norm_g, k_norm_g, ret_decay_logit, ret_norm_g, w_proj_a, w_proj_b,
                            w_out)
        yp = _mixer_layer(yp, bp, sp, mod3, layer, 0, lw, tab_p, e_bd)
        ys = _mixer_layer(ys, bs, ss, mod3, layer, bp, lw, tab_s, e_bd)
    return yp.reshape(bp, sp, D_MODEL), ys.reshape(bs, ss, D_MODEL)
```

```python
import functools
import math

import jax
import jax.numpy as jnp
from jax import lax
from jax.experimental import pallas as pl
from jax.experimental.pallas import tpu as pltpu

D_MODEL = 1024
DEPTH = 4
A_HEADS = 8
A_HEAD_DIM = 64
A_WIDTH = A_HEADS * A_HEAD_DIM
A_ROT_HALF = A_HEAD_DIM // 8
ATT_THETA = 500000.0
DILATIONS = (1, 4, 16)
HALF_WIN = 64
R_HEADS = 4
R_KEY_DIM = 128
R_VAL_DIM = 256
R_V_WIDTH = R_HEADS * R_VAL_DIM
RET_THETA = 10000.0
RET_CHUNK = 128
IN_WIDTH = 4 * A_WIDTH + 2 * R_HEADS * R_KEY_DIM + 2 * R_V_WIDTH + 2 * D_MODEL
EPS = 1e-6
NEG = -1e30
LOG2E = 1.4426950408889634

LANES = 128
COL = 512
N_COL = IN_WIDTH // COL
N_CHUNK = COL // LANES
E_BLK = 256
CB_QA, CB_KA, CB_VA, CB_ZA, CB_QR, CB_KR = 0, 1, 2, 3, 4, 5
OFF_VR, OFF_ZR, OFF_GA, OFF_GR = 3072, 4096, 5120, 6144
QKV_W = 3 * A_WIDTH
ROPE_TABLE_W = 5 * LANES

TM_IN = 512
TM_OUT = 512
OUT_SUB = 128
TQ_ATT = 1024
Q_SUB = 128
K_WIN = Q_SUB + 2 * HALF_WIN
ATT_UNROLL = 8
RET_BLOCK = 2048
VMEM_LIMIT = 56 * 1024 * 1024

F32 = jnp.float32
BF16 = jnp.bfloat16


def _silu(v):
    return v * jax.nn.sigmoid(v)


def _params(sem, vmem=VMEM_LIMIT):
    return pltpu.CompilerParams(dimension_semantics=sem, vmem_limit_bytes=vmem)


def _mod_kernel(c_ref, w_ref, b_ref, o_ref):
    s = _silu(c_ref[...])
    o_ref[...] = jnp.dot(s, w_ref[...], preferred_element_type=F32, precision=lax.Precision.HIGHEST) + b_ref[...]


def _modulation(c_all, w_ada, b_ada):
    nb = c_all.shape[0]
    return pl.pallas_call(
        _mod_kernel,
        grid=(DEPTH, 3),
        in_specs=[
            pl.BlockSpec((nb, D_MODEL), lambda l, j: (0, 0)),
            pl.BlockSpec((None, D_MODEL, D_MODEL), lambda l, j: (l, 0, j)),
            pl.BlockSpec((None, 1, D_MODEL), lambda l, j: (l, 0, j)),
        ],
        out_specs=pl.BlockSpec((None, nb, D_MODEL), lambda l, j: (l, 0, j)),
        out_shape=jax.ShapeDtypeStruct((DEPTH, nb, 3 * D_MODEL), F32),
        compiler_params=_params(("parallel", "parallel")),
        name="adaln_mod",
    )(c_all, w_ada, b_ada.reshape(DEPTH, 1, 3 * D_MODEL))


def _rope_tables(seq):
    pos = jnp.arange(seq, dtype=F32)
    fa = jnp.exp(-math.log(ATT_THETA) * jnp.arange(A_ROT_HALF, dtype=F32) / A_ROT_HALF)
    ang = pos[:, None] * fa[None, :]
    cos, sin = jnp.cos(ang), jnp.sin(ang)
    rest = A_HEAD_DIM - 2 * A_ROT_HALF
    one = jnp.ones((seq, rest), F32)
    zero = jnp.zeros((seq, rest), F32)
    z8 = jnp.zeros((seq, A_ROT_HALF), F32)
    att_c = jnp.tile(jnp.concatenate([cos, cos, one], 1), (1, 2))
    att_a = jnp.tile(jnp.concatenate([-sin, z8, zero], 1), (1, 2))
    att_b = jnp.tile(jnp.concatenate([z8, sin, zero], 1), (1, 2))
    half = R_KEY_DIM // 2
    fr = jnp.exp(-math.log(RET_THETA) * jnp.arange(half, dtype=F32) / half)
    ang = pos[:, None] * fr[None, :]
    cos, sin = jnp.cos(ang), jnp.sin(ang)
    ret_c = jnp.concatenate([cos, cos], 1)
    ret_s = jnp.concatenate([-sin, sin], 1)
    return jnp.concatenate([att_c, att_a, att_b, ret_c, ret_s], 1)


def _inproj_kernel(x_ref, mod_ref, g_ref, w_ref, gq_ref, gk_ref, e_ref, tab_ref, o_ref, a4_ref, a16_ref, rows_scr):
    x = x_ref[...]
    ms = jnp.mean(x * x, axis=-1, keepdims=True)
    y = x * lax.rsqrt(ms + EPS) * g_ref[...]
    h = (y * (1.0 + mod_ref[1:2, :]) + mod_ref[0:1, :]).astype(BF16)

    att_c = tab_ref[:, 0 * LANES:1 * LANES]
    att_a = tab_ref[:, 1 * LANES:2 * LANES]
    att_b = tab_ref[:, 2 * LANES:3 * LANES]
    ret_c = tab_ref[:, 3 * LANES:4 * LANES]
    ret_s = tab_ref[:, 4 * LANES:5 * LANES]

    def emit_qkv(j, c, val):
        lo = j * COL + c * LANES
        o_ref[:, lo:lo + LANES] = val.astype(BF16)
        rows_scr[0, c] = val
        n4 = TM_IN // 4
        n16 = TM_IN // 16
        for r4 in range(4):
            by4 = rows_scr[0, c, pl.ds(r4, n4, stride=4), :]
            a4_ref[r4, :, lo:lo + LANES] = by4.astype(BF16)
            rows_scr[1, c, r4 * n4:(r4 + 1) * n4, :] = by4
        for r4 in range(4):
            for r in range(4):
                by16 = rows_scr[1, c, pl.ds(r4 * n4 + r, n16, stride=4), :]
                a16_ref[4 * r + r4, :, lo:lo + LANES] = by16.astype(BF16)

    for j in range(N_COL):
        acc = jnp.dot(h, w_ref[:, j * COL:(j + 1) * COL], preferred_element_type=F32)
        if j in (CB_QA, CB_KA):
            sq = (acc * acc).astype(BF16)
            ss = jnp.concatenate(
                [jnp.dot(sq[:, s * E_BLK:(s + 1) * E_BLK], e_ref[...], preferred_element_type=F32)
                 for s in range(COL // E_BLK)], axis=-1)
            gain = gq_ref[...] if j == CB_QA else gk_ref[...]
            yn = acc * lax.rsqrt(ss * (1.0 / A_HEAD_DIM) + EPS) * gain
            for c in range(N_CHUNK):
                xc = yn[:, c * LANES:(c + 1) * LANES]
                rot = (xc * att_c + pltpu.roll(xc, LANES - A_ROT_HALF, 1) * att_a
                       + pltpu.roll(xc, A_ROT_HALF, 1) * att_b)
                emit_qkv(j, c, rot)
        elif j == CB_VA:
            for c in range(N_CHUNK):
                emit_qkv(j, c, acc[:, c * LANES:(c + 1) * LANES])
        elif j in (CB_QR, CB_KR):
            for c in range(N_CHUNK):
                xc = acc[:, c * LANES:(c + 1) * LANES]
                rot = xc * ret_c + pltpu.roll(xc, R_KEY_DIM // 2, 1) * ret_s
                if j == CB_KR:
                    rot = rot * (R_KEY_DIM ** -0.5)
                o_ref[:, j * COL + c * LANES:j * COL + (c + 1) * LANES] = rot.astype(BF16)
        else:
            o_ref[:, j * COL:(j + 1) * COL] = acc.astype(BF16)


def _inproj(x2, batch, seq, mod3, layer, b_off, norm_g, w_in, gq, gk, e_bd, tables):
    t = x2.shape[0]
    per_seq = seq // TM_IN
    const = dict(pipeline_mode=pl.Buffered(1))

    def dilated(dil):
        return pl.BlockSpec((None, dil, TM_IN // dil, QKV_W), lambda i: (i // per_seq, 0, i % per_seq, 0))

    return pl.pallas_call(
        _inproj_kernel,
        grid=(t // TM_IN,),
        in_specs=[
            pl.BlockSpec((TM_IN, D_MODEL), lambda i: (i, 0)),
            pl.BlockSpec((None, None, 3, D_MODEL), lambda i: (layer, i // per_seq + b_off, 0, 0)),
            pl.BlockSpec((1, D_MODEL), lambda i: (0, 0)),
            pl.BlockSpec((D_MODEL, IN_WIDTH), lambda i: (0, 0), **const),
            pl.BlockSpec((1, COL), lambda i: (0, 0)),
            pl.BlockSpec((1, COL), lambda i: (0, 0)),
            pl.BlockSpec((E_BLK, E_BLK), lambda i: (0, 0), **const),
            pl.BlockSpec((TM_IN, ROPE_TABLE_W), lambda i: (i % per_seq, 0)),
        ],
        out_specs=[pl.BlockSpec((TM_IN, IN_WIDTH), lambda i: (i, 0)), dilated(4), dilated(16)],
        out_shape=[
            jax.ShapeDtypeStruct((t, IN_WIDTH), BF16),
            jax.ShapeDtypeStruct((batch, 4, seq // 4, QKV_W), BF16),
            jax.ShapeDtypeStruct((batch, 16, seq // 16, QKV_W), BF16),
        ],
        scratch_shapes=[pltpu.VMEM((2, N_CHUNK, TM_IN, LANES), F32)],
        compiler_params=_params(("parallel",)),
        name="in_proj",
    )(x2, mod3, norm_g, w_in, gq, gk, e_bd, tables)


def _attn_kernel(q_ref, kp_ref, kc_ref, kn_ref, vp_ref, vc_ref, vn_ref, o_ref, st_ref, kbuf, vbuf, *, tq, nr, length):
    kbuf[:, 0:HALF_WIN, :] = kp_ref[...]
    kbuf[:, HALF_WIN:HALF_WIN + tq, :] = kc_ref[...]
    kbuf[:, HALF_WIN + tq:, :] = kn_ref[...]
    key_lane = lax.broadcasted_iota(jnp.int32, (nr, tq + 2 * HALF_WIN, LANES), 2)
    for p in range(A_WIDTH // LANES):
        src = slice(p * LANES, (p + 1) * LANES)
        dst = slice(2 * p * LANES, (2 * p + 1) * LANES)
        vbuf[:, 0:HALF_WIN, dst] = vp_ref[:, :, src]
        vbuf[:, HALF_WIN:HALF_WIN + tq, dst] = vc_ref[:, :, src]
        vbuf[:, HALF_WIN + tq:, dst] = vn_ref[:, :, src]
        den_lanes = (key_lane == A_HEADS + 2 * p) | (key_lane == A_HEADS + 2 * p + 1)
        vbuf[:, :, (2 * p + 1) * LANES:(2 * p + 2) * LANES] = jnp.where(den_lanes, 1.0, 0.0).astype(BF16)
    q0 = pl.program_id(2) * tq

    kj = lax.broadcasted_iota(jnp.int32, (Q_SUB, K_WIN), 1)
    qi = lax.broadcasted_iota(jnp.int32, (Q_SUB, K_WIN), 0)
    band = (kj >= qi) & (kj <= qi + 2 * HALF_WIN)
    lane = lax.broadcasted_iota(jnp.int32, (Q_SUB, LANES), 1)
    first_head = lane < A_HEAD_DIM
    odd_lane = (lane & 1) == 1
    sub_per_res = tq // Q_SUB

    def sub_block(t, carry):
        ri = t // sub_per_res
        r0 = pl.multiple_of((t % sub_per_res) * Q_SUB, Q_SUB)
        kpos = kj + (q0 + r0 - HALF_WIN)
        bias = jnp.where(band & (kpos >= 0) & (kpos < length), 0.0, NEG)
        bias = jnp.concatenate([bias, bias], axis=0)
        stat = jnp.zeros((Q_SUB, LANES), F32)
        for p in range(A_WIDTH // LANES):
            cols = slice(p * LANES, (p + 1) * LANES)
            q2 = q_ref[ri, pl.ds(r0, Q_SUB), cols]
            k2 = kbuf[ri, pl.ds(r0, K_WIN), cols]
            v2 = vbuf[ri, pl.ds(r0, K_WIN), 2 * p * LANES:(2 * p + 2) * LANES]
            zero = jnp.zeros_like(q2)
            qs = jnp.concatenate([jnp.where(first_head, q2, zero), jnp.where(first_head, zero, q2)], axis=0)
            sc = lax.dot_general(qs, k2, (((1,), (1,)), ((), ())), preferred_element_type=F32) + bias
            m = jnp.max(sc, axis=-1, keepdims=True)
            pr = jnp.exp2(sc - m)
            pvd = jnp.dot(pr.astype(BF16), v2, preferred_element_type=F32)
            pv, den = pvd[:, :LANES], pvd[:, LANES:]
            stat = stat + jnp.where(odd_lane, den[Q_SUB:], den[:Q_SUB])
            stat = jnp.where(lane == 2 * p, m[:Q_SUB], stat)
            stat = jnp.where(lane == 2 * p + 1, m[Q_SUB:], stat)
            o_ref[ri, pl.ds(r0, Q_SUB), cols] = jnp.where(first_head, pv[:Q_SUB], pv[Q_SUB:]).astype(o_ref.dtype)
        st_ref[ri, pl.ds(r0, Q_SUB), :] = stat
        return carry

    lax.fori_loop(0, nr * sub_per_res, sub_block, 0, unroll=ATT_UNROLL)


def _attention(qkv, dil):
    b, _, length, _ = qkv.shape
    tq = min(TQ_ATT, length)
    nr = min(dil, TQ_ATT // tq)
    nq = length // tq
    halo_per_q = tq // HALF_WIN
    last_halo = length // HALF_WIN - 1

    def cur(cb):
        return pl.BlockSpec((None, nr, tq, COL), lambda bi, r, i: (bi, r, i, cb))

    def prev(cb):
        return pl.BlockSpec((None, nr, HALF_WIN, COL),
                            lambda bi, r, i: (bi, r, jnp.maximum(i * halo_per_q - 1, 0), cb))

    def nxt(cb):
        return pl.BlockSpec((None, nr, HALF_WIN, COL),
                            lambda bi, r, i: (bi, r, jnp.minimum((i + 1) * halo_per_q, last_halo), cb))

    return pl.pallas_call(
        functools.partial(_attn_kernel, tq=tq, nr=nr, length=length),
        grid=(b, dil // nr, nq),
        in_specs=[cur(CB_QA), prev(CB_KA), cur(CB_KA), nxt(CB_KA), prev(CB_VA), cur(CB_VA), nxt(CB_VA)],
        out_specs=[
            pl.BlockSpec((None, nr, tq, A_WIDTH), lambda bi, r, i: (bi, r, i, 0)),
            pl.BlockSpec((None, nr, tq, LANES), lambda bi, r, i: (bi, r, i, 0)),
        ],
        out_shape=[
            jax.ShapeDtypeStruct((b, dil, length, A_WIDTH), BF16),
            jax.ShapeDtypeStruct((b, dil, length, LANES), F32),
        ],
        scratch_shapes=[pltpu.VMEM((nr, tq + 2 * HALF_WIN, COL), BF16),
                        pltpu.VMEM((nr, tq + 2 * HALF_WIN, 2 * COL), BF16)],
        compiler_params=_params(("parallel", "parallel", "parallel")),
        name=f"window_attn_d{dil}",
    )(qkv, qkv, qkv, qkv, qkv, qkv, qkv)


def _log_sigmoid(v):
    return jnp.minimum(v, 0.0) - jnp.log1p(jnp.exp(-jnp.abs(v)))


def _decays(dl_ref, head):
    lg_f = _log_sigmoid(dl_ref[pl.ds(head, 1), :])
    lg_b = _log_sigmoid(dl_ref[pl.ds(head + R_HEADS, 1), :])
    return lg_f, lg_b


def _chunk_tables(lg_f, lg_b):
    c = RET_CHUNK
    row = lax.broadcasted_iota(jnp.int32, (c, c), 0).astype(F32)
    col = lax.broadcasted_iota(jnp.int32, (c, c), 1).astype(F32)
    rel = row - col
    intra = (jnp.where(rel >= 0, jnp.exp(jnp.maximum(rel, 0.0) * lg_f), 0.0)
             + jnp.where(rel <= 0, jnp.exp(jnp.maximum(-rel, 0.0) * lg_b), 0.0))
    kd_f = jnp.exp((c - 1 - row) * lg_f)
    kd_b = jnp.exp(row * lg_b)
    qd_f = jnp.exp((row + 1.0) * lg_f)
    qd_b = jnp.exp((c - row) * lg_b)
    cd_f = jnp.exp(c * lg_f)
    cd_b = jnp.exp(c * lg_b)
    return intra, kd_f, kd_b, qd_f, qd_b, cd_f, cd_b


def _wide(v):
    return jnp.concatenate([v, v], axis=-1)


def _delta(k_chunk, kd, v_chunk):
    kt = (k_chunk.astype(F32) * kd).T.astype(BF16)
    return jnp.dot(kt, v_chunk, preferred_element_type=F32)


def _ret_state_kernel(dl_ref, kf_ref, vf_ref, kb_ref, vb_ref, fin_ref, bin_ref, sf, sb):
    head = pl.program_id(0)

    @pl.when(pl.program_id(1) == 0)
    def _():
        sf[...] = jnp.zeros_like(sf)
        sb[...] = jnp.zeros_like(sb)

    fin_ref[...] = sf[...]
    bin_ref[...] = sb[...]
    lg_f, lg_b = _decays(dl_ref, head)
    _, kd_f, kd_b, _, _, cd_f, cd_b = _chunk_tables(lg_f, lg_b)
    cd_f, cd_b = _wide(cd_f), _wide(cd_b)
    n_chunks = RET_BLOCK // RET_CHUNK

    def scan(t, carry):
        rf = pl.ds(pl.multiple_of(t * RET_CHUNK, RET_CHUNK), RET_CHUNK)
        rb = pl.ds(pl.multiple_of((n_chunks - 1 - t) * RET_CHUNK, RET_CHUNK), RET_CHUNK)
        sf[...] = sf[...] * cd_f + _delta(kf_ref[rf, :], kd_f, vf_ref[rf, :])
        sb[...] = sb[...] * cd_b + _delta(kb_ref[rb, :], kd_b, vb_ref[rb, :])
        return carry

    lax.fori_loop(0, n_chunks, scan, 0, unroll=2)


def _retention_states(p3, dl_tile):
    nb = p3.shape[0]
    k_cb = (CB_KR * COL) // R_KEY_DIM
    v_cb = OFF_VR // R_VAL_DIM
    shape = jax.ShapeDtypeStruct((nb, R_HEADS, R_KEY_DIM, R_VAL_DIM), F32)
    return pl.pallas_call(
        _ret_state_kernel,
        grid=(R_HEADS, nb),
        in_specs=[
            pl.BlockSpec((2 * R_HEADS, LANES), lambda h, j: (0, 0)),
            pl.BlockSpec((None, RET_BLOCK, R_KEY_DIM), lambda h, j: (j, 0, k_cb + h)),
            pl.BlockSpec((None, RET_BLOCK, R_VAL_DIM), lambda h, j: (j, 0, v_cb + h)),
            pl.BlockSpec((None, RET_BLOCK, R_KEY_DIM), lambda h, j: (nb - 1 - j, 0, k_cb + h)),
            pl.BlockSpec((None, RET_BLOCK, R_VAL_DIM), lambda h, j: (nb - 1 - j, 0, v_cb + h)),
        ],
        out_specs=[
            pl.BlockSpec((None, None, R_KEY_DIM, R_VAL_DIM), lambda h, j: (j, h, 0, 0)),
            pl.BlockSpec((None, None, R_KEY_DIM, R_VAL_DIM), lambda h, j: (nb - 1 - j, h, 0, 0)),
        ],
        out_shape=[shape, shape],
        scratch_shapes=[pltpu.VMEM((R_KEY_DIM, R_VAL_DIM), F32), pltpu.VMEM((R_KEY_DIM, R_VAL_DIM), F32)],
        compiler_params=_params(("parallel", "arbitrary")),
        name="retention_states",
    )(dl_tile, p3, p3, p3, p3)


def _ret_kernel(*refs, has_state):
    if has_state:
        dl_ref, gn_ref, q_ref, k_ref, v_ref, z_ref, fin_ref, bin_ref, o_ref, sf, sb, s_all = refs
    else:
        dl_ref, gn_ref, q_ref, k_ref, v_ref, z_ref, o_ref, sf, sb, s_all = refs
    head = pl.program_id(1)
    lg_f, lg_b = _decays(dl_ref, head)
    intra, kd_f, kd_b, qd_f, qd_b, cd_f, cd_b = _chunk_tables(lg_f, lg_b)
    cd_f, cd_b = _wide(cd_f), _wide(cd_b)
    gn = gn_ref[...]
    if has_state:
        sf[...] = fin_ref[...]
        sb[...] = bin_ref[...]
    else:
        sf[...] = jnp.zeros_like(sf)
        sb[...] = jnp.zeros_like(sb)
    n_chunks = RET_BLOCK // RET_CHUNK

    def scan(t, carry):
        tb = n_chunks - 1 - t
        rf = pl.ds(pl.multiple_of(t * RET_CHUNK, RET_CHUNK), RET_CHUNK)
        rb = pl.ds(pl.multiple_of(tb * RET_CHUNK, RET_CHUNK), RET_CHUNK)
        state_f = sf[...]
        s_all[t, 0:R_KEY_DIM, :] = state_f.astype(BF16)
        sf[...] = state_f * cd_f + _delta(k_ref[rf, :], kd_f, v_ref[rf, :])
        state_b = sb[...]
        s_all[tb, R_KEY_DIM:2 * R_KEY_DIM, :] = state_b.astype(BF16)
        sb[...] = state_b * cd_b + _delta(k_ref[rb, :], kd_b, v_ref[rb, :])
        return carry

    def emit(n, carry):
        rows = pl.ds(pl.multiple_of(n * RET_CHUNK, RET_CHUNK), RET_CHUNK)
        q, k, v = q_ref[rows, :], k_ref[rows, :], v_ref[rows, :]
        qf = q.astype(F32)
        qq = jnp.concatenate([(qf * qd_f).astype(BF16), (qf * qd_b).astype(BF16)], axis=-1)
        sc = lax.dot_general(q, k, (((1,), (1,)), ((), ())), preferred_element_type=F32) * intra
        y = (jnp.dot(sc.astype(BF16), v, preferred_element_type=F32)
             + jnp.dot(qq, s_all[n], preferred_element_type=F32))
        mu = jnp.mean(y, axis=-1, keepdims=True)
        yc = y - mu
        var = jnp.mean(yc * yc, axis=-1, keepdims=True)
        yn = yc * lax.rsqrt(var + EPS) * gn
        o_ref[rows, :] = (yn * _silu(z_ref[rows, :].astype(F32))).astype(BF16)
        return carry

    lax.fori_loop(0, n_chunks, scan, 0, unroll=8)
    lax.fori_loop(0, n_chunks, emit, 0, unroll=8)


def _retention(p3, dl_tile, gn, states):
    nb = p3.shape[0]
    q_cb = (CB_QR * COL) // R_KEY_DIM
    k_cb = (CB_KR * COL) // R_KEY_DIM
    v_cb = OFF_VR // R_VAL_DIM
    z_cb = OFF_ZR // R_VAL_DIM
    in_specs = [
        pl.BlockSpec((2 * R_HEADS, LANES), lambda b, h: (0, 0)),
        pl.BlockSpec((1, R_VAL_DIM), lambda b, h: (0, h)),
        pl.BlockSpec((None, RET_BLOCK, R_KEY_DIM), lambda b, h: (b, 0, q_cb + h)),
        pl.BlockSpec((None, RET_BLOCK, R_KEY_DIM), lambda b, h: (b, 0, k_cb + h)),
        pl.BlockSpec((None, RET_BLOCK, R_VAL_DIM), lambda b, h: (b, 0, v_cb + h)),
        pl.BlockSpec((None, RET_BLOCK, R_VAL_DIM), lambda b, h: (b, 0, z_cb + h)),
    ]
    args = [dl_tile, gn, p3, p3, p3, p3]
    if states is not None:
        st = pl.BlockSpec((None, None, R_KEY_DIM, R_VAL_DIM), lambda b, h: (b, h, 0, 0))
        in_specs += [st, st]
        args += list(states)
    n_chunks = RET_BLOCK // RET_CHUNK
    out = pl.pallas_call(
        functools.partial(_ret_kernel, has_state=states is not None),
        grid=(nb, R_HEADS),
        in_specs=in_specs,
        out_specs=pl.BlockSpec((None, RET_BLOCK, R_VAL_DIM), lambda b, h: (b, 0, h)),
        out_shape=jax.ShapeDtypeStruct((nb, RET_BLOCK, R_V_WIDTH), BF16),
        scratch_shapes=[
            pltpu.VMEM((R_KEY_DIM, R_VAL_DIM), F32),
            pltpu.VMEM((R_KEY_DIM, R_VAL_DIM), F32),
            pltpu.VMEM((n_chunks, 2 * R_KEY_DIM, R_VAL_DIM), BF16),
        ],
        compiler_params=_params(("parallel", "parallel")),
        name="retention",
    )(*args)
    return out.reshape(nb * RET_BLOCK, R_V_WIDTH)


def _out_kernel(x_ref, mod_ref, o1_ref, o4_ref, o16_ref, l1_ref, l4_ref, l16_ref, za_ref, ur_ref, ga_ref, gr_ref,
                wa_ref, wb_ref, wo_ref, y_ref, o_scr, l_scr):
    tm = x_ref.shape[0]
    for gi, (dil, o_ref, l_ref) in enumerate(((4, o4_ref, l4_ref), (16, o16_ref, l16_ref))):
        n = tm // dil
        for r in range(dil):
            l_scr[gi, pl.ds(r, n, stride=dil), :] = l_ref[r]
            for c in range(N_CHUNK):
                o_scr[gi, c, pl.ds(r, n, stride=dil), :] = o_ref[r, :, c * LANES:(c + 1) * LANES].astype(F32)

    lane = lax.broadcasted_iota(jnp.int32, (OUT_SUB, LANES), 1)
    first_head = lane < A_HEAD_DIM
    for s in range(tm // OUT_SUB):
        rows = slice(s * OUT_SUB, (s + 1) * OUT_SUB)
        stats = (l1_ref[rows, :], l_scr[0, rows, :], l_scr[1, rows, :])
        top = jnp.maximum(jnp.maximum(stats[0], stats[1]), stats[2])
        es = [jnp.exp2(v - top) for v in stats]
        dens = [pltpu.roll(v, LANES - A_HEADS, 1) for v in stats]
        total = es[0] * dens[0] + es[1] * dens[1] + es[2] * dens[2]
        total = jnp.where(lane < A_HEADS, total, 1.0)
        ws = [e / total for e in es]
        ua = []
        for c in range(N_CHUNK):
            cols = slice(c * LANES, (c + 1) * LANES)
            outs = (o1_ref[rows, cols].astype(F32), o_scr[0, c, rows, :], o_scr[1, c, rows, :])
            ya = jnp.zeros((OUT_SUB, LANES), F32)
            for g in range(3):
                w_pair = jnp.where(first_head, ws[g][:, 2 * c:2 * c + 1], ws[g][:, 2 * c + 1:2 * c + 2])
                ya = ya + w_pair * outs[g]
            ua.append((ya * _silu(za_ref[rows, cols].astype(F32))).astype(BF16))
        ua = jnp.concatenate(ua, axis=-1)
        ya_p = jnp.dot(ua, wa_ref[...], preferred_element_type=F32)
        yr_p = jnp.dot(ur_ref[rows, :], wb_ref[...], preferred_element_type=F32)
        merged = (jax.nn.sigmoid(ga_ref[rows, :].astype(F32)) * ya_p
                  + jax.nn.sigmoid(gr_ref[rows, :].astype(F32)) * yr_p)
        out = jnp.dot(merged.astype(BF16), wo_ref[...], preferred_element_type=F32)
        y_ref[rows, :] = x_ref[rows, :] + mod_ref[2:3, :] * out


def _output(x2, mod3, layer, b_off, seq, attn, p2, ur, wa, wb, wo):
    t = x2.shape[0]
    per_seq = seq // TM_OUT
    const = dict(pipeline_mode=pl.Buffered(1))
    (o1, l1), (o4, l4), (o16, l16) = attn

    def rows(width, cb=0):
        return pl.BlockSpec((TM_OUT, width), lambda i: (i, cb))

    def dilated(dil, width):
        return pl.BlockSpec((None, dil, TM_OUT // dil, width), lambda i: (i // per_seq, 0, i % per_seq, 0))

    return pl.pallas_call(
        _out_kernel,
        grid=(t // TM_OUT,),
        in_specs=[
            rows(D_MODEL),
            pl.BlockSpec((None, None, 3, D_MODEL), lambda i: (layer, i // per_seq + b_off, 0, 0)),
            rows(A_WIDTH), dilated(4, A_WIDTH), dilated(16, A_WIDTH),
            rows(LANES), dilated(4, LANES), dilated(16, LANES),
            rows(A_WIDTH, CB_ZA),
            rows(R_V_WIDTH),
            rows(D_MODEL, OFF_GA // D_MODEL),
            rows(D_MODEL, OFF_GR // D_MODEL),
            pl.BlockSpec((A_WIDTH, D_MODEL), lambda i: (0, 0), **const),
            pl.BlockSpec((R_V_WIDTH, D_MODEL), lambda i: (0, 0), **const),
            pl.BlockSpec((D_MODEL, D_MODEL), lambda i: (0, 0), **const),
        ],
        out_specs=rows(D_MODEL),
        out_shape=jax.ShapeDtypeStruct((t, D_MODEL), F32),
        scratch_shapes=[pltpu.VMEM((2, N_CHUNK, TM_OUT, LANES), F32), pltpu.VMEM((2, TM_OUT, LANES), F32)],
        compiler_params=_params(("parallel",)),
        name="merge_out_proj",
    )(x2, mod3, o1.reshape(t, A_WIDTH), o4, o16, l1.reshape(t, LANES), l4, l16, p2, ur, p2, p2, wa, wb, wo)


def _mixer_layer(x2, batch, seq, mod3, layer, b_off, lw, tables, e_bd):
    p2, a4, a16 = _inproj(x2, batch, seq, mod3, layer, b_off, lw["norm_g"], lw["w_in"], lw["gq"], lw["gk"], e_bd,
                          tables)
    attn = [_attention(qkv, dil) for qkv, dil in zip((p2.reshape(batch, 1, seq, IN_WIDTH), a4, a16), DILATIONS)]
    nb = batch * seq // RET_BLOCK
    pr = p2.reshape(nb, RET_BLOCK, IN_WIDTH)
    states = _retention_states(pr, lw["dl"]) if seq > RET_BLOCK else None
    ur = _retention(pr, lw["dl"], lw["gn"], states)
    return _output(x2, mod3, layer, b_off, seq, attn, p2, ur, lw["wa"], lw["wb"], lw["wo"])


def _layer_weights(layer, norm_g, w_in, q_norm_g, k_norm_g, ret_decay_logit, ret_norm_g, w_proj_a, w_proj_b, w_out):
    return {
        "norm_g": norm_g[layer].reshape(1, D_MODEL),
        "w_in": w_in[layer].astype(BF16),
        "gq": (jnp.tile(q_norm_g[layer], A_HEADS) * (A_HEAD_DIM ** -0.5 * LOG2E)).reshape(1, A_WIDTH),
        "gk": jnp.tile(k_norm_g[layer], A_HEADS).reshape(1, A_WIDTH),
        "dl": jnp.broadcast_to(ret_decay_logit[layer].astype(F32).reshape(2 * R_HEADS, 1), (2 * R_HEADS, LANES)),
        "gn": ret_norm_g[layer].reshape(1, R_V_WIDTH),
        "wa": w_proj_a[layer].astype(BF16),
        "wb": w_proj_b[layer].astype(BF16),
        "wo": w_out[layer].astype(BF16),
    }


def kernel(x_prompt, x_sample, c_prompt, c_sample, norm_g, w_ada, b_ada, w_in, q_norm_g, k_norm_g, ret_decay_logit,
           ret_norm_g, w_proj_a, w_proj_b, w_out):
    bp, sp, _ = x_prompt.shape
    bs, ss, _ = x_sample.shape
    pad = (-(bp + bs)) % 8
    c_all = jnp.concatenate([c_prompt, c_sample, jnp.zeros((pad, D_MODEL), F32)], axis=0)
    mod3 = _modulation(c_all, w_ada, b_ada).reshape(DEPTH, bp + bs + pad, 3, D_MODEL)
    head_of_lane = jnp.arange(E_BLK) // A_HEAD_DIM
    e_bd = (head_of_lane[:, None] == head_of_lane[None, :]).astype(BF16)
    tab_p = _rope_tables(sp)
    tab_s = tab_p if ss == sp else _rope_tables(ss)
    yp = x_prompt.reshape(bp * sp, D_MODEL)
    ys = x_sample.reshape(bs * ss, D_MODEL)
    for layer in range(DEPTH):
        lw = _layer_weights(layer, norm_g, w_in, q_norm_g, k_norm_g, ret_decay_logit, ret_norm_g, w_proj_a, w_proj_b,
                            w_out)
        yp = _mixer_layer(yp, bp, sp, mod3, layer, 0, lw, tab_p, e_bd)
        ys = _mixer_layer(ys, bs, ss, mod3, layer, bp, lw, tab_s, e_bd)
    return yp.reshape(bp, sp, D_MODEL), ys.reshape(bs, ss, D_MODEL)
```

```python
import functools
import math

import jax
import jax.numpy as jnp
from jax import lax
from jax.experimental import pallas as pl
from jax.experimental.pallas import tpu as pltpu

D_MODEL = 1024
DEPTH = 4
A_HEADS = 8
A_HEAD_DIM = 64
A_WIDTH = A_HEADS * A_HEAD_DIM
A_ROT_HALF = A_HEAD_DIM // 8
ATT_THETA = 500000.0
DILATIONS = (1, 4, 16)
HALF_WIN = 64
R_HEADS = 4
R_KEY_DIM = 128
R_VAL_DIM = 256
R_V_WIDTH = R_HEADS * R_VAL_DIM
RET_THETA = 10000.0
RET_CHUNK = 128
IN_WIDTH = 4 * A_WIDTH + 2 * R_HEADS * R_KEY_DIM + 2 * R_V_WIDTH + 2 * D_MODEL
EPS = 1e-6
NEG = -1e30
LOG2E = 1.4426950408889634

LANES = 128
COL = 512
N_COL = IN_WIDTH // COL
N_CHUNK = COL // LANES
E_BLK = 256
CB_QA, CB_KA, CB_VA, CB_ZA, CB_QR, CB_KR = 0, 1, 2, 3, 4, 5
OFF_VR, OFF_ZR, OFF_GA, OFF_GR = 3072, 4096, 5120, 6144
QKV_W = 3 * A_WIDTH
ROPE_TABLE_W = 5 * LANES

TM_IN = 512
TM_OUT = 1024
OUT_SUB = 128
TQ_ATT = 1024
Q_SUB = 128
K_WIN = Q_SUB + 2 * HALF_WIN
ATT_UNROLL = 8
RET_BLOCK = 2048
VMEM_LIMIT = 56 * 1024 * 1024

F32 = jnp.float32
BF16 = jnp.bfloat16


def _silu(v):
    h = 0.5 * v
    return h + h * jnp.tanh(h)


def _times_silu_of_half(u, h):
    uh = u * h
    return uh + uh * jnp.tanh(h)


def _params(sem, vmem=VMEM_LIMIT):
    return pltpu.CompilerParams(dimension_semantics=sem, vmem_limit_bytes=vmem)


def _mod_kernel(c_ref, w_ref, b_ref, o_ref):
    s = _silu(c_ref[...])
    o_ref[...] = jnp.dot(s, w_ref[...], preferred_element_type=F32, precision=lax.Precision.HIGHEST) + b_ref[...]


def _modulation(c_all, w_ada, b_ada):
    nb = c_all.shape[0]
    return pl.pallas_call(
        _mod_kernel,
        grid=(DEPTH, 3),
        in_specs=[
            pl.BlockSpec((nb, D_MODEL), lambda l, j: (0, 0)),
            pl.BlockSpec((None, D_MODEL, D_MODEL), lambda l, j: (l, 0, j)),
            pl.BlockSpec((None, 1, D_MODEL), lambda l, j: (l, 0, j)),
        ],
        out_specs=pl.BlockSpec((None, nb, D_MODEL), lambda l, j: (l, 0, j)),
        out_shape=jax.ShapeDtypeStruct((DEPTH, nb, 3 * D_MODEL), F32),
        compiler_params=_params(("parallel", "parallel")),
        name="adaln_mod",
    )(c_all, w_ada, b_ada.reshape(DEPTH, 1, 3 * D_MODEL))


def _rope_tables(seq):
    pos = jnp.arange(seq, dtype=F32)
    fa = jnp.exp(-math.log(ATT_THETA) * jnp.arange(A_ROT_HALF, dtype=F32) / A_ROT_HALF)
    ang = pos[:, None] * fa[None, :]
    cos, sin = jnp.cos(ang), jnp.sin(ang)
    rest = A_HEAD_DIM - 2 * A_ROT_HALF
    one = jnp.ones((seq, rest), F32)
    zero = jnp.zeros((seq, rest), F32)
    z8 = jnp.zeros((seq, A_ROT_HALF), F32)
    att_c = jnp.tile(jnp.concatenate([cos, cos, one], 1), (1, 2))
    att_a = jnp.tile(jnp.concatenate([-sin, z8, zero], 1), (1, 2))
    att_b = jnp.tile(jnp.concatenate([z8, sin, zero], 1), (1, 2))
    half = R_KEY_DIM // 2
    fr = jnp.exp(-math.log(RET_THETA) * jnp.arange(half, dtype=F32) / half)
    ang = pos[:, None] * fr[None, :]
    cos, sin = jnp.cos(ang), jnp.sin(ang)
    ret_c = jnp.concatenate([cos, cos], 1)
    ret_s = jnp.concatenate([-sin, sin], 1)
    return jnp.concatenate([att_c, att_a, att_b, ret_c, ret_s], 1)


def _inproj_kernel(x_ref, mod_ref, g_ref, w_ref, gq_ref, gk_ref, e_ref, tab_ref, o_ref, a4_ref, a16_ref, rows_scr):
    x = x_ref[...]
    ms = jnp.mean(x * x, axis=-1, keepdims=True)
    y = x * lax.rsqrt(ms + EPS) * g_ref[...]
    h = (y * (1.0 + mod_ref[1:2, :]) + mod_ref[0:1, :]).astype(BF16)

    att_c = tab_ref[:, 0 * LANES:1 * LANES]
    att_a = tab_ref[:, 1 * LANES:2 * LANES]
    att_b = tab_ref[:, 2 * LANES:3 * LANES]
    ret_c = tab_ref[:, 3 * LANES:4 * LANES]
    ret_s = tab_ref[:, 4 * LANES:5 * LANES]

    def emit_qkv(j, c, val):
        lo = j * COL + c * LANES
        o_ref[:, lo:lo + LANES] = val.astype(BF16)
        rows_scr[0, c] = val
        n4 = TM_IN // 4
        n16 = TM_IN // 16
        for r4 in range(4):
            by4 = rows_scr[0, c, pl.ds(r4, n4, stride=4), :]
            a4_ref[r4, :, lo:lo + LANES] = by4.astype(BF16)
            rows_scr[1, c, r4 * n4:(r4 + 1) * n4, :] = by4
        for r4 in range(4):
            for r in range(4):
                by16 = rows_scr[1, c, pl.ds(r4 * n4 + r, n16, stride=4), :]
                a16_ref[4 * r + r4, :, lo:lo + LANES] = by16.astype(BF16)

    for j in range(N_COL):
        acc = jnp.dot(h, w_ref[:, j * COL:(j + 1) * COL], preferred_element_type=F32)
        if j in (CB_QA, CB_KA):
            sq = (acc * acc).astype(BF16)
            ss = jnp.concatenate(
                [jnp.dot(sq[:, s * E_BLK:(s + 1) * E_BLK], e_ref[...], preferred_element_type=F32)
                 for s in range(COL // E_BLK)], axis=-1)
            gain = gq_ref[...] if j == CB_QA else gk_ref[...]
            yn = acc * lax.rsqrt(ss * (1.0 / A_HEAD_DIM) + EPS) * gain
            for c in range(N_CHUNK):
                xc = yn[:, c * LANES:(c + 1) * LANES]
                rot = (xc * att_c + pltpu.roll(xc, LANES - A_ROT_HALF, 1) * att_a
                       + pltpu.roll(xc, A_ROT_HALF, 1) * att_b)
                emit_qkv(j, c, rot)
        elif j == CB_VA:
            for c in range(N_CHUNK):
                emit_qkv(j, c, acc[:, c * LANES:(c + 1) * LANES])
        elif j in (CB_QR, CB_KR):
            for c in range(N_CHUNK):
                xc = acc[:, c * LANES:(c + 1) * LANES]
                rot = xc * ret_c + pltpu.roll(xc, R_KEY_DIM // 2, 1) * ret_s
                if j == CB_KR:
                    rot = rot * (R_KEY_DIM ** -0.5)
                o_ref[:, j * COL + c * LANES:j * COL + (c + 1) * LANES] = rot.astype(BF16)
        else:
            o_ref[:, j * COL:(j + 1) * COL] = acc.astype(BF16)


def _inproj(x2, batch, seq, mod3, layer, b_off, norm_g, w_in, gq, gk, e_bd, tables):
    t = x2.shape[0]
    per_seq = seq // TM_IN
    const = dict(pipeline_mode=pl.Buffered(1))

    def dilated(dil):
        return pl.BlockSpec((None, dil, TM_IN // dil, QKV_W), lambda i: (i // per_seq, 0, i % per_seq, 0))

    return pl.pallas_call(
        _inproj_kernel,
        grid=(t // TM_IN,),
        in_specs=[
            pl.BlockSpec((TM_IN, D_MODEL), lambda i: (i, 0)),
            pl.BlockSpec((None, None, 3, D_MODEL), lambda i: (layer, i // per_seq + b_off, 0, 0)),
            pl.BlockSpec((1, D_MODEL), lambda i: (0, 0)),
            pl.BlockSpec((D_MODEL, IN_WIDTH), lambda i: (0, 0), **const),
            pl.BlockSpec((1, COL), lambda i: (0, 0)),
            pl.BlockSpec((1, COL), lambda i: (0, 0)),
            pl.BlockSpec((E_BLK, E_BLK), lambda i: (0, 0), **const),
            pl.BlockSpec((TM_IN, ROPE_TABLE_W), lambda i: (i % per_seq, 0)),
        ],
        out_specs=[pl.BlockSpec((TM_IN, IN_WIDTH), lambda i: (i, 0)), dilated(4), dilated(16)],
        out_shape=[
            jax.ShapeDtypeStruct((t, IN_WIDTH), BF16),
            jax.ShapeDtypeStruct((batch, 4, seq // 4, QKV_W), BF16),
            jax.ShapeDtypeStruct((batch, 16, seq // 16, QKV_W), BF16),
        ],
        scratch_shapes=[pltpu.VMEM((2, N_CHUNK, TM_IN, LANES), F32)],
        compiler_params=_params(("parallel",)),
        name="in_proj",
    )(x2, mod3, norm_g, w_in, gq, gk, e_bd, tables)


def _attn_kernel(q_ref, kp_ref, kc_ref, kn_ref, vp_ref, vc_ref, vn_ref, o_ref, st_ref, kbuf, vbuf, *, tq, nr, length):
    kbuf[:, 0:HALF_WIN, :] = kp_ref[...]
    kbuf[:, HALF_WIN:HALF_WIN + tq, :] = kc_ref[...]
    kbuf[:, HALF_WIN + tq:, :] = kn_ref[...]
    key_lane = lax.broadcasted_iota(jnp.int32, (nr, tq + 2 * HALF_WIN, LANES), 2)
    for p in range(A_WIDTH // LANES):
        src = slice(p * LANES, (p + 1) * LANES)
        dst = slice(2 * p * LANES, (2 * p + 1) * LANES)
        vbuf[:, 0:HALF_WIN, dst] = vp_ref[:, :, src]
        vbuf[:, HALF_WIN:HALF_WIN + tq, dst] = vc_ref[:, :, src]
        vbuf[:, HALF_WIN + tq:, dst] = vn_ref[:, :, src]
        den_lanes = (key_lane == A_HEADS + 2 * p) | (key_lane == A_HEADS + 2 * p + 1)
        vbuf[:, :, (2 * p + 1) * LANES:(2 * p + 2) * LANES] = jnp.where(den_lanes, 1.0, 0.0).astype(BF16)
    q0 = pl.program_id(2) * tq

    kj = lax.broadcasted_iota(jnp.int32, (Q_SUB, K_WIN), 1)
    qi = lax.broadcasted_iota(jnp.int32, (Q_SUB, K_WIN), 0)
    band = (kj >= qi) & (kj <= qi + 2 * HALF_WIN)
    lane = lax.broadcasted_iota(jnp.int32, (Q_SUB, LANES), 1)
    first_head = lane < A_HEAD_DIM
    odd_lane = (lane & 1) == 1
    sub_per_res = tq // Q_SUB

    def sub_block(t, carry):
        ri = t // sub_per_res
        r0 = pl.multiple_of((t % sub_per_res) * Q_SUB, Q_SUB)
        kpos = kj + (q0 + r0 - HALF_WIN)
        bias = jnp.where(band & (kpos >= 0) & (kpos < length), 0.0, NEG)
        bias = jnp.concatenate([bias, bias], axis=0)
        stat = jnp.zeros((Q_SUB, LANES), F32)
        for p in range(A_WIDTH // LANES):
            cols = slice(p * LANES, (p + 1) * LANES)
            q2 = q_ref[ri, pl.ds(r0, Q_SUB), cols]
            k2 = kbuf[ri, pl.ds(r0, K_WIN), cols]
            v2 = vbuf[ri, pl.ds(r0, K_WIN), 2 * p * LANES:(2 * p + 2) * LANES]
            zero = jnp.zeros_like(q2)
            qs = jnp.concatenate([jnp.where(first_head, q2, zero), jnp.where(first_head, zero, q2)], axis=0)
            sc = lax.dot_general(qs, k2, (((1,), (1,)), ((), ())), preferred_element_type=F32) + bias
            m = jnp.max(sc, axis=-1, keepdims=True)
            pr = jnp.exp2(sc - m)
            pvd = jnp.dot(pr.astype(BF16), v2, preferred_element_type=F32)
            pv, den = pvd[:, :LANES], pvd[:, LANES:]
            stat = stat + jnp.where(odd_lane, den[Q_SUB:], den[:Q_SUB])
            stat = jnp.where(lane == 2 * p, m[:Q_SUB], stat)
            stat = jnp.where(lane == 2 * p + 1, m[Q_SUB:], stat)
            o_ref[ri, pl.ds(r0, Q_SUB), cols] = jnp.where(first_head, pv[:Q_SUB], pv[Q_SUB:]).astype(o_ref.dtype)
        st_ref[ri, pl.ds(r0, Q_SUB), :] = stat
        return carry

    lax.fori_loop(0, nr * sub_per_res, sub_block, 0, unroll=ATT_UNROLL)


def _attention(qkv, dil):
    b, _, length, _ = qkv.shape
    tq = min(TQ_ATT, length)
    nr = min(dil, TQ_ATT // tq)
    nq = length // tq
    halo_per_q = tq // HALF_WIN
    last_halo = length // HALF_WIN - 1

    def cur(cb):
        return pl.BlockSpec((None, nr, tq, COL), lambda bi, r, i: (bi, r, i, cb))

    def prev(cb):
        return pl.BlockSpec((None, nr, HALF_WIN, COL),
                            lambda bi, r, i: (bi, r, jnp.maximum(i * halo_per_q - 1, 0), cb))

    def nxt(cb):
        return pl.BlockSpec((None, nr, HALF_WIN, COL),
                            lambda bi, r, i: (bi, r, jnp.minimum((i + 1) * halo_per_q, last_halo), cb))

    return pl.pallas_call(
        functools.partial(_attn_kernel, tq=tq, nr=nr, length=length),
        grid=(b, dil // nr, nq),
        in_specs=[cur(CB_QA), prev(CB_KA), cur(CB_KA), nxt(CB_KA), prev(CB_VA), cur(CB_VA), nxt(CB_VA)],
        out_specs=[
            pl.BlockSpec((None, nr, tq, A_WIDTH), lambda bi, r, i: (bi, r, i, 0)),
            pl.BlockSpec((None, nr, tq, LANES), lambda bi, r, i: (bi, r, i, 0)),
        ],
        out_shape=[
            jax.ShapeDtypeStruct((b, dil, length, A_WIDTH), BF16),
            jax.ShapeDtypeStruct((b, dil, length, LANES), F32),
        ],
        scratch_shapes=[pltpu.VMEM((nr, tq + 2 * HALF_WIN, COL), BF16),
                        pltpu.VMEM((nr, tq + 2 * HALF_WIN, 2 * COL), BF16)],
        compiler_params=_params(("parallel", "parallel", "parallel")),
        name=f"window_attn_d{dil}",
    )(qkv, qkv, qkv, qkv, qkv, qkv, qkv)


def _log_sigmoid(v):
    return jnp.minimum(v, 0.0) - jnp.log1p(jnp.exp(-jnp.abs(v)))


def _decays(dl_ref, head):
    lg_f = _log_sigmoid(dl_ref[pl.ds(head, 1), :])
    lg_b = _log_sigmoid(dl_ref[pl.ds(head + R_HEADS, 1), :])
    return lg_f, lg_b


def _chunk_tables(lg_f, lg_b):
    c = RET_CHUNK
    row = lax.broadcasted_iota(jnp.int32, (c, c), 0).astype(F32)
    col = lax.broadcasted_iota(jnp.int32, (c, c), 1).astype(F32)
    rel = row - col
    intra = (jnp.where(rel >= 0, jnp.exp(jnp.maximum(rel, 0.0) * lg_f), 0.0)
             + jnp.where(rel <= 0, jnp.exp(jnp.maximum(-rel, 0.0) * lg_b), 0.0))
    kd_f = jnp.exp((c - 1 - row) * lg_f)
    kd_b = jnp.exp(row * lg_b)
    qd_f = jnp.exp((row + 1.0) * lg_f)
    qd_b = jnp.exp((c - row) * lg_b)
    cd_f = jnp.exp(c * lg_f)
    cd_b = jnp.exp(c * lg_b)
    return intra, kd_f, kd_b, qd_f, qd_b, cd_f, cd_b


def _wide(v):
    return jnp.concatenate([v, v], axis=-1)


def _delta(k_chunk, kd, v_chunk):
    kt = (k_chunk.astype(F32) * kd).T.astype(BF16)
    return jnp.dot(kt, v_chunk, preferred_element_type=F32)


def _ret_state_kernel(dl_ref, kf_ref, vf_ref, kb_ref, vb_ref, fin_ref, bin_ref, sf, sb):
    head = pl.program_id(0)

    @pl.when(pl.program_id(1) == 0)
    def _():
        sf[...] = jnp.zeros_like(sf)
        sb[...] = jnp.zeros_like(sb)

    fin_ref[...] = sf[...]
    bin_ref[...] = sb[...]
    lg_f, lg_b = _decays(dl_ref, head)
    _, kd_f, kd_b, _, _, cd_f, cd_b = _chunk_tables(lg_f, lg_b)
    cd_f, cd_b = _wide(cd_f), _wide(cd_b)
    n_chunks = RET_BLOCK // RET_CHUNK

    def scan(t, carry):
        rf = pl.ds(pl.multiple_of(t * RET_CHUNK, RET_CHUNK), RET_CHUNK)
        rb = pl.ds(pl.multiple_of((n_chunks - 1 - t) * RET_CHUNK, RET_CHUNK), RET_CHUNK)
        sf[...] = sf[...] * cd_f + _delta(kf_ref[rf, :], kd_f, vf_ref[rf, :])
        sb[...] = sb[...] * cd_b + _delta(kb_ref[rb, :], kd_b, vb_ref[rb, :])
        return carry

    lax.fori_loop(0, n_chunks, scan, 0, unroll=True)


def _retention_states(p3, dl_tile):
    nb = p3.shape[0]
    k_cb = (CB_KR * COL) // R_KEY_DIM
    v_cb = OFF_VR // R_VAL_DIM
    shape = jax.ShapeDtypeStruct((nb, R_HEADS, R_KEY_DIM, R_VAL_DIM), F32)
    return pl.pallas_call(
        _ret_state_kernel,
        grid=(R_HEADS, nb),
        in_specs=[
            pl.BlockSpec((2 * R_HEADS, LANES), lambda h, j: (0, 0)),
            pl.BlockSpec((None, RET_BLOCK, R_KEY_DIM), lambda h, j: (j, 0, k_cb + h)),
            pl.BlockSpec((None, RET_BLOCK, R_VAL_DIM), lambda h, j: (j, 0, v_cb + h)),
            pl.BlockSpec((None, RET_BLOCK, R_KEY_DIM), lambda h, j: (nb - 1 - j, 0, k_cb + h)),
            pl.BlockSpec((None, RET_BLOCK, R_VAL_DIM), lambda h, j: (nb - 1 - j, 0, v_cb + h)),
        ],
        out_specs=[
            pl.BlockSpec((None, None, R_KEY_DIM, R_VAL_DIM), lambda h, j: (j, h, 0, 0)),
            pl.BlockSpec((None, None, R_KEY_DIM, R_VAL_DIM), lambda h, j: (nb - 1 - j, h, 0, 0)),
        ],
        out_shape=[shape, shape],
        scratch_shapes=[pltpu.VMEM((R_KEY_DIM, R_VAL_DIM), F32), pltpu.VMEM((R_KEY_DIM, R_VAL_DIM), F32)],
        compiler_params=_params(("parallel", "arbitrary")),
        name="retention_states",
    )(dl_tile, p3, p3, p3, p3)


def _ret_kernel(*refs, has_state):
    if has_state:
        dl_ref, gn_ref, q_ref, k_ref, v_ref, z_ref, fin_ref, bin_ref, o_ref, sf, sb, s_all = refs
    else:
        dl_ref, gn_ref, q_ref, k_ref, v_ref, z_ref, o_ref, sf, sb, s_all = refs
    head = pl.program_id(1)
    lg_f, lg_b = _decays(dl_ref, head)
    intra, kd_f, kd_b, qd_f, qd_b, cd_f, cd_b = _chunk_tables(lg_f, lg_b)
    cd_f, cd_b = _wide(cd_f), _wide(cd_b)
    gn = gn_ref[...]
    if has_state:
        sf[...] = fin_ref[...]
        sb[...] = bin_ref[...]
    else:
        sf[...] = jnp.zeros_like(sf)
        sb[...] = jnp.zeros_like(sb)
    n_chunks = RET_BLOCK // RET_CHUNK

    def scan(t, carry):
        tb = n_chunks - 1 - t
        rf = pl.ds(pl.multiple_of(t * RET_CHUNK, RET_CHUNK), RET_CHUNK)
        rb = pl.ds(pl.multiple_of(tb * RET_CHUNK, RET_CHUNK), RET_CHUNK)
        state_f = sf[...]
        s_all[t, 0:R_KEY_DIM, :] = state_f.astype(BF16)
        sf[...] = state_f * cd_f + _delta(k_ref[rf, :], kd_f, v_ref[rf, :])
        state_b = sb[...]
        s_all[tb, R_KEY_DIM:2 * R_KEY_DIM, :] = state_b.astype(BF16)
        sb[...] = state_b * cd_b + _delta(k_ref[rb, :], kd_b, v_ref[rb, :])
        return carry

    def emit(n, carry):
        rows = pl.ds(pl.multiple_of(n * RET_CHUNK, RET_CHUNK), RET_CHUNK)
        q, k, v = q_ref[rows, :], k_ref[rows, :], v_ref[rows, :]
        qf = q.astype(F32)
        qq = jnp.concatenate([(qf * qd_f).astype(BF16), (qf * qd_b).astype(BF16)], axis=-1)
        sc = lax.dot_general(q, k, (((1,), (1,)), ((), ())), preferred_element_type=F32) * intra
        y = (jnp.dot(sc.astype(BF16), v, preferred_element_type=F32)
             + jnp.dot(qq, s_all[n], preferred_element_type=F32))
        mu = jnp.mean(y, axis=-1, keepdims=True)
        yc = y - mu
        var = jnp.mean(yc * yc, axis=-1, keepdims=True)
        yn = yc * lax.rsqrt(var + EPS) * gn
        o_ref[rows, :] = _times_silu_of_half(yn, z_ref[rows, :].astype(F32)).astype(BF16)
        return carry

    lax.fori_loop(0, n_chunks, scan, 0, unroll=True)
    lax.fori_loop(0, n_chunks, emit, 0, unroll=True)


def _retention(p3, dl_tile, gn, states):
    nb = p3.shape[0]
    q_cb = (CB_QR * COL) // R_KEY_DIM
    k_cb = (CB_KR * COL) // R_KEY_DIM
    v_cb = OFF_VR // R_VAL_DIM
    z_cb = OFF_ZR // R_VAL_DIM
    in_specs = [
        pl.BlockSpec((2 * R_HEADS, LANES), lambda b, h: (0, 0)),
        pl.BlockSpec((1, R_VAL_DIM), lambda b, h: (0, h)),
        pl.BlockSpec((None, RET_BLOCK, R_KEY_DIM), lambda b, h: (b, 0, q_cb + h)),
        pl.BlockSpec((None, RET_BLOCK, R_KEY_DIM), lambda b, h: (b, 0, k_cb + h)),
        pl.BlockSpec((None, RET_BLOCK, R_VAL_DIM), lambda b, h: (b, 0, v_cb + h)),
        pl.BlockSpec((None, RET_BLOCK, R_VAL_DIM), lambda b, h: (b, 0, z_cb + h)),
    ]
    args = [dl_tile, gn, p3, p3, p3, p3]
    if states is not None:
        st = pl.BlockSpec((None, None, R_KEY_DIM, R_VAL_DIM), lambda b, h: (b, h, 0, 0))
        in_specs += [st, st]
        args += list(states)
    n_chunks = RET_BLOCK // RET_CHUNK
    out = pl.pallas_call(
        functools.partial(_ret_kernel, has_state=states is not None),
        grid=(nb, R_HEADS),
        in_specs=in_specs,
        out_specs=pl.BlockSpec((None, RET_BLOCK, R_VAL_DIM), lambda b, h: (b, 0, h)),
        out_shape=jax.ShapeDtypeStruct((nb, RET_BLOCK, R_V_WIDTH), BF16),
        scratch_shapes=[
            pltpu.VMEM((R_KEY_DIM, R_VAL_DIM), F32),
            pltpu.VMEM((R_KEY_DIM, R_VAL_DIM), F32),
            pltpu.VMEM((n_chunks, 2 * R_KEY_DIM, R_VAL_DIM), BF16),
        ],
        compiler_params=_params(("parallel", "parallel")),
        name="retention",
    )(*args)
    return out.reshape(nb * RET_BLOCK, R_V_WIDTH)


def _out_kernel(x_ref, mod_ref, o1_ref, o4_ref, o16_ref, l1_ref, l4_ref, l16_ref, za_ref, ur_ref, ga_ref, gr_ref,
                wa_ref, wb_ref, wo_ref, y_ref, o_scr, l_scr):
    tm = x_ref.shape[0]
    for gi, (dil, o_ref, l_ref) in enumerate(((4, o4_ref, l4_ref), (16, o16_ref, l16_ref))):
        n = tm // dil
        for r in range(dil):
            l_scr[gi, pl.ds(r, n, stride=dil), :] = l_ref[r]
            for c in range(N_CHUNK):
                o_scr[gi, c, pl.ds(r, n, stride=dil), :] = o_ref[r, :, c * LANES:(c + 1) * LANES].astype(F32)

    lane = lax.broadcasted_iota(jnp.int32, (OUT_SUB, LANES), 1)
    first_head = lane < A_HEAD_DIM
    for s in range(tm // OUT_SUB):
        rows = slice(s * OUT_SUB, (s + 1) * OUT_SUB)
        stats = (l1_ref[rows, :], l_scr[0, rows, :], l_scr[1, rows, :])
        top = jnp.maximum(jnp.maximum(stats[0], stats[1]), stats[2])
        es = [jnp.exp2(v - top) for v in stats]
        dens = [pltpu.roll(v, LANES - A_HEADS, 1) for v in stats]
        total = es[0] * dens[0] + es[1] * dens[1] + es[2] * dens[2]
        total = jnp.where(lane < A_HEADS, total, 1.0)
        ws = [e / total for e in es]
        ua = []
        for c in range(N_CHUNK):
            cols = slice(c * LANES, (c + 1) * LANES)
            outs = (o1_ref[rows, cols].astype(F32), o_scr[0, c, rows, :], o_scr[1, c, rows, :])
            ya = jnp.zeros((OUT_SUB, LANES), F32)
            for g in range(3):
                w_pair = jnp.where(first_head, ws[g][:, 2 * c:2 * c + 1], ws[g][:, 2 * c + 1:2 * c + 2])
                ya = ya + w_pair * outs[g]
            ua.append(_times_silu_of_half(ya, za_ref[rows, cols].astype(F32)).astype(BF16))
        ua = jnp.concatenate(ua, axis=-1)
        ya_p = jnp.dot(ua, wa_ref[...], preferred_element_type=F32)
        yr_p = jnp.dot(ur_ref[rows, :], wb_ref[...], preferred_element_type=F32)
        ta = jnp.tanh(ga_ref[rows, :].astype(F32))
        tr = jnp.tanh(gr_ref[rows, :].astype(F32))
        merged2 = (ya_p + ya_p * ta) + (yr_p + yr_p * tr)
        out = jnp.dot(merged2.astype(BF16), wo_ref[...], preferred_element_type=F32)
        y_ref[rows, :] = x_ref[rows, :] + (0.5 * mod_ref[2:3, :]) * out


def _output(x2, mod3, layer, b_off, seq, attn, p2, ur, wa, wb, wo):
    t = x2.shape[0]
    per_seq = seq // TM_OUT
    const = dict(pipeline_mode=pl.Buffered(1))
    (o1, l1), (o4, l4), (o16, l16) = attn

    def rows(width, cb=0):
        return pl.BlockSpec((TM_OUT, width), lambda i: (i, cb))

    def dilated(dil, width):
        return pl.BlockSpec((None, dil, TM_OUT // dil, width), lambda i: (i // per_seq, 0, i % per_seq, 0))

    return pl.pallas_call(
        _out_kernel,
        grid=(t // TM_OUT,),
        in_specs=[
            rows(D_MODEL),
            pl.BlockSpec((None, None, 3, D_MODEL), lambda i: (layer, i // per_seq + b_off, 0, 0)),
            rows(A_WIDTH), dilated(4, A_WIDTH), dilated(16, A_WIDTH),
            rows(LANES), dilated(4, LANES), dilated(16, LANES),
            rows(A_WIDTH, CB_ZA),
            rows(R_V_WIDTH),
            rows(D_MODEL, OFF_GA // D_MODEL),
            rows(D_MODEL, OFF_GR // D_MODEL),
            pl.BlockSpec((A_WIDTH, D_MODEL), lambda i: (0, 0), **const),
            pl.BlockSpec((R_V_WIDTH, D_MODEL), lambda i: (0, 0), **const),
            pl.BlockSpec((D_MODEL, D_MODEL), lambda i: (0, 0), **const),
        ],
        out_specs=rows(D_MODEL),
        out_shape=jax.ShapeDtypeStruct((t, D_MODEL), F32),
        scratch_shapes=[pltpu.VMEM((2, N_CHUNK, TM_OUT, LANES), F32), pltpu.VMEM((2, TM_OUT, LANES), F32)],
        compiler_params=_params(("parallel",)),
        name="merge_out_proj",
    )(x2, mod3, o1.reshape(t, A_WIDTH), o4, o16, l1.reshape(t, LANES), l4, l16, p2, ur, p2, p2, wa, wb, wo)


def _mixer_layer(x2, batch, seq, mod3, layer, b_off, lw, tables, e_bd):
    p2, a4, a16 = _inproj(x2, batch, seq, mod3, layer, b_off, lw["norm_g"], lw["w_in"], lw["gq"], lw["gk"], e_bd,
                          tables)
    attn = [_attention(qkv, dil) for qkv, dil in zip((p2.reshape(batch, 1, seq, IN_WIDTH), a4, a16), DILATIONS)]
    nb = batch * seq // RET_BLOCK
    pr = p2.reshape(nb, RET_BLOCK, IN_WIDTH)
    states = _retention_states(pr, lw["dl"]) if seq > RET_BLOCK else None
    ur = _retention(pr, lw["dl"], lw["gn"], states)
    return _output(x2, mod3, layer, b_off, seq, attn, p2, ur, lw["wa"], lw["wb"], lw["wo"])


def _half_gate_columns():
    col = jnp.arange(IN_WIDTH)
    za = (col >= CB_ZA * COL) & (col < (CB_ZA + 1) * COL)
    return jnp.where(za | (col >= OFF_ZR), 0.5, 1.0).astype(F32)[None, :]


def _layer_weights(layer,norm_g, w_in, q_norm_g, k_norm_g, ret_decay_logit, ret_norm_g, w_proj_a, w_proj_b, w_out):
    return {
        "norm_g": norm_g[layer].reshape(1, D_MODEL),
        "w_in": (w_in[layer] * _half_gate_columns()).astype(BF16),
        "gq": (jnp.tile(q_norm_g[layer], A_HEADS) * (A_HEAD_DIM ** -0.5 * LOG2E)).reshape(1, A_WIDTH),
        "gk": jnp.tile(k_norm_g[layer], A_HEADS).reshape(1, A_WIDTH),
        "dl": jnp.broadcast_to(ret_decay_logit[layer].astype(F32).reshape(2 * R_HEADS, 1), (2 * R_HEADS, LANES)),
        "gn": ret_norm_g[layer].reshape(1, R_V_WIDTH),
        "wa": w_proj_a[layer].astype(BF16),
        "wb": w_proj_b[layer].astype(BF16),
        "wo": w_out[layer].astype(BF16),
    }


def kernel(x_prompt, x_sample, c_prompt, c_sample, norm_g, w_ada, b_ada, w_in, q_norm_g, k_norm_g, ret_decay_logit,
           ret_norm_g, w_proj_a, w_proj_b, w_out):
    bp, sp, _ = x_prompt.shape
    bs, ss, _ = x_sample.shape
    pad = (-(bp + bs)) % 8
    c_all = jnp.concatenate([c_prompt, c_sample, jnp.zeros((pad, D_MODEL), F32)], axis=0)
    mod3 = _modulation(c_all, w_ada, b_ada).reshape(DEPTH, bp + bs + pad, 3, D_MODEL)
    head_of_lane = jnp.arange(E_BLK) // A_HEAD_DIM
    e_bd = (head_of_lane[:, None] == head_of_lane[None, :]).astype(BF16)
    tab_p = _rope_tables(sp)
    tab_s = tab_p if ss == sp else _rope_tables(ss)
    yp = x_prompt.reshape(bp * sp, D_MODEL)
    ys = x_sample.reshape(bs * ss, D_MODEL)
    for layer in range(DEPTH):
        lw = _layer_weights(layer, norm_g, w_in, q_norm_g, k_norm_g, ret_decay_logit, ret_norm_g, w_proj_a, w_proj_b,
                            w_out)
        yp = _mixer_layer(yp, bp, sp, mod3, layer, 0, lw, tab_p, e_bd)
        ys = _mixer_layer(ys, bs, ss, mod3, layer, bp, lw, tab_s, e_bd)
    return yp.reshape(bp, sp, D_MODEL), ys.reshape(bs, ss, D_MODEL)
```

```python
import functools
import math

import jax
import jax.numpy as jnp
from jax import lax
from jax.experimental import pallas as pl
from jax.experimental.pallas import tpu as pltpu

D_MODEL = 1024
DEPTH = 4
A_HEADS = 8
A_HEAD_DIM = 64
A_WIDTH = A_HEADS * A_HEAD_DIM
A_ROT_HALF = A_HEAD_DIM // 8
ATT_THETA = 500000.0
DILATIONS = (1, 4, 16)
HALF_WIN = 64
R_HEADS = 4
R_KEY_DIM = 128
R_VAL_DIM = 256
R_V_WIDTH = R_HEADS * R_VAL_DIM
RET_THETA = 10000.0
RET_CHUNK = 128
IN_WIDTH = 4 * A_WIDTH + 2 * R_HEADS * R_KEY_DIM + 2 * R_V_WIDTH + 2 * D_MODEL
EPS = 1e-6
NEG = -1e30
LOG2E = 1.4426950408889634

LANES = 128
COL = 512
N_COL = IN_WIDTH // COL
N_CHUNK = COL // LANES
E_BLK = 256
CB_QA, CB_KA, CB_VA, CB_ZA, CB_QR, CB_KR = 0, 1, 2, 3, 4, 5
OFF_VR, OFF_ZR, OFF_GA, OFF_GR = 3072, 4096, 5120, 6144
QKV_W = 3 * A_WIDTH
ROPE_TABLE_W = 5 * LANES

TM_IN = 512
TM_OUT = 1024
OUT_SUB = 256
TQ_ATT = 2048
Q_SUB = 128
K_WIN = Q_SUB + 2 * HALF_WIN
ATT_UNROLL = 16
RET_BLOCK = 2048
VMEM_LIMIT = 56 * 1024 * 1024

F32 = jnp.float32
BF16 = jnp.bfloat16


def _silu(v):
    h = 0.5 * v
    return h + h * jnp.tanh(h)


def _times_silu_of_half(u, h):
    uh = u * h
    return uh + uh * jnp.tanh(h)


def _params(sem, vmem=VMEM_LIMIT):
    return pltpu.CompilerParams(dimension_semantics=sem, vmem_limit_bytes=vmem)


def _mod_kernel(c_ref, w_ref, b_ref, o_ref):
    s = _silu(c_ref[...])
    o_ref[...] = jnp.dot(s, w_ref[...], preferred_element_type=F32, precision=lax.Precision.HIGHEST) + b_ref[...]


def _modulation(c_all, w_ada, b_ada):
    nb = c_all.shape[0]
    return pl.pallas_call(
        _mod_kernel,
        grid=(DEPTH, 3),
        in_specs=[
            pl.BlockSpec((nb, D_MODEL), lambda l, j: (0, 0)),
            pl.BlockSpec((None, D_MODEL, D_MODEL), lambda l, j: (l, 0, j)),
            pl.BlockSpec((None, 1, D_MODEL), lambda l, j: (l, 0, j)),
        ],
        out_specs=pl.BlockSpec((None, nb, D_MODEL), lambda l, j: (l, 0, j)),
        out_shape=jax.ShapeDtypeStruct((DEPTH, nb, 3 * D_MODEL), F32),
        compiler_params=_params(("parallel", "parallel")),
        name="adaln_mod",
    )(c_all, w_ada, b_ada.reshape(DEPTH, 1, 3 * D_MODEL))


def _rope_tables(seq):
    pos = jnp.arange(seq, dtype=F32)
    fa = jnp.exp(-math.log(ATT_THETA) * jnp.arange(A_ROT_HALF, dtype=F32) / A_ROT_HALF)
    ang = pos[:, None] * fa[None, :]
    cos, sin = jnp.cos(ang), jnp.sin(ang)
    rest = A_HEAD_DIM - 2 * A_ROT_HALF
    one = jnp.ones((seq, rest), F32)
    zero = jnp.zeros((seq, rest), F32)
    z8 = jnp.zeros((seq, A_ROT_HALF), F32)
    att_c = jnp.tile(jnp.concatenate([cos, cos, one], 1), (1, 2))
    att_a = jnp.tile(jnp.concatenate([-sin, z8, zero], 1), (1, 2))
    att_b = jnp.tile(jnp.concatenate([z8, sin, zero], 1), (1, 2))
    half = R_KEY_DIM // 2
    fr = jnp.exp(-math.log(RET_THETA) * jnp.arange(half, dtype=F32) / half)
    ang = pos[:, None] * fr[None, :]
    cos, sin = jnp.cos(ang), jnp.sin(ang)
    ret_c = jnp.concatenate([cos, cos], 1)
    ret_s = jnp.concatenate([-sin, sin], 1)
    return jnp.concatenate([att_c, att_a, att_b, ret_c, ret_s], 1)


def _inproj_kernel(x_ref, mod_ref, g_ref, w_ref, gq_ref, gk_ref, e_ref, tab_ref, o_ref, a4_ref, a16_ref, rows_scr):
    x = x_ref[...]
    ms = jnp.mean(x * x, axis=-1, keepdims=True)
    y = x * lax.rsqrt(ms + EPS) * g_ref[...]
    h = (y * (1.0 + mod_ref[1:2, :]) + mod_ref[0:1, :]).astype(BF16)

    att_c = tab_ref[:, 0 * LANES:1 * LANES]
    att_a = tab_ref[:, 1 * LANES:2 * LANES]
    att_b = tab_ref[:, 2 * LANES:3 * LANES]
    ret_c = tab_ref[:, 3 * LANES:4 * LANES]
    ret_s = tab_ref[:, 4 * LANES:5 * LANES]

    def emit_qkv(j, c, val):
        lo = j * COL + c * LANES
        o_ref[:, lo:lo + LANES] = val.astype(BF16)
        rows_scr[0, c] = val
        n4 = TM_IN // 4
        n16 = TM_IN // 16
        for r4 in range(4):
            by4 = rows_scr[0, c, pl.ds(r4, n4, stride=4), :]
            a4_ref[r4, :, lo:lo + LANES] = by4.astype(BF16)
            rows_scr[1, c, r4 * n4:(r4 + 1) * n4, :] = by4
        for r4 in range(4):
            for r in range(4):
                by16 = rows_scr[1, c, pl.ds(r4 * n4 + r, n16, stride=4), :]
                a16_ref[4 * r + r4, :, lo:lo + LANES] = by16.astype(BF16)

    for j in range(N_COL):
        acc = jnp.dot(h, w_ref[:, j * COL:(j + 1) * COL], preferred_element_type=F32)
        if j in (CB_QA, CB_KA):
            sq = (acc * acc).astype(BF16)
            ss = jnp.concatenate(
                [jnp.dot(sq[:, s * E_BLK:(s + 1) * E_BLK], e_ref[...], preferred_element_type=F32)
                 for s in range(COL // E_BLK)], axis=-1)
            gain = gq_ref[...] if j == CB_QA else gk_ref[...]
            yn = acc * lax.rsqrt(ss * (1.0 / A_HEAD_DIM) + EPS) * gain
            for c in range(N_CHUNK):
                xc = yn[:, c * LANES:(c + 1) * LANES]
                rot = (xc * att_c + pltpu.roll(xc, LANES - A_ROT_HALF, 1) * att_a
                       + pltpu.roll(xc, A_ROT_HALF, 1) * att_b)
                emit_qkv(j, c, rot)
        elif j == CB_VA:
            for c in range(N_CHUNK):
                emit_qkv(j, c, acc[:, c * LANES:(c + 1) * LANES])
        elif j in (CB_QR, CB_KR):
            for c in range(N_CHUNK):
                xc = acc[:, c * LANES:(c + 1) * LANES]
                rot = xc * ret_c + pltpu.roll(xc, R_KEY_DIM // 2, 1) * ret_s
                if j == CB_KR:
                    rot = rot * (R_KEY_DIM ** -0.5)
                o_ref[:, j * COL + c * LANES:j * COL + (c + 1) * LANES] = rot.astype(BF16)
        else:
            o_ref[:, j * COL:(j + 1) * COL] = acc.astype(BF16)


def _inproj(x2, batch, seq, mod3, layer, b_off, norm_g, w_in, gq, gk, e_bd, tables):
    t = x2.shape[0]
    per_seq = seq // TM_IN
    const = dict(pipeline_mode=pl.Buffered(1))

    def dilated(dil):
        return pl.BlockSpec((None, dil, TM_IN // dil, QKV_W), lambda i: (i // per_seq, 0, i % per_seq, 0))

    return pl.pallas_call(
        _inproj_kernel,
        grid=(t // TM_IN,),
        in_specs=[
            pl.BlockSpec((TM_IN, D_MODEL), lambda i: (i, 0)),
            pl.BlockSpec((None, None, 3, D_MODEL), lambda i: (layer, i // per_seq + b_off, 0, 0)),
            pl.BlockSpec((1, D_MODEL), lambda i: (0, 0)),
            pl.BlockSpec((D_MODEL, IN_WIDTH), lambda i: (0, 0), **const),
            pl.BlockSpec((1, COL), lambda i: (0, 0)),
            pl.BlockSpec((1, COL), lambda i: (0, 0)),
            pl.BlockSpec((E_BLK, E_BLK), lambda i: (0, 0), **const),
            pl.BlockSpec((TM_IN, ROPE_TABLE_W), lambda i: (i % per_seq, 0)),
        ],
        out_specs=[pl.BlockSpec((TM_IN, IN_WIDTH), lambda i: (i, 0)), dilated(4), dilated(16)],
        out_shape=[
            jax.ShapeDtypeStruct((t, IN_WIDTH), BF16),
            jax.ShapeDtypeStruct((batch, 4, seq // 4, QKV_W), BF16),
            jax.ShapeDtypeStruct((batch, 16, seq // 16, QKV_W), BF16),
        ],
        scratch_shapes=[pltpu.VMEM((2, N_CHUNK, TM_IN, LANES), F32)],
        compiler_params=_params(("parallel",)),
        name="in_proj",
    )(x2, mod3, norm_g, w_in, gq, gk, e_bd, tables)


def _attn_kernel(q_ref, kp_ref, kc_ref, kn_ref, vp_ref, vc_ref, vn_ref, o_ref, st_ref, kbuf, vbuf, *, tq, nr, length):
    kbuf[:, 0:HALF_WIN, :] = kp_ref[...]
    kbuf[:, HALF_WIN:HALF_WIN + tq, :] = kc_ref[...]
    kbuf[:, HALF_WIN + tq:, :] = kn_ref[...]
    key_lane = lax.broadcasted_iota(jnp.int32, (nr, tq + 2 * HALF_WIN, LANES), 2)
    for p in range(A_WIDTH // LANES):
        src = slice(p * LANES, (p + 1) * LANES)
        dst = slice(2 * p * LANES, (2 * p + 1) * LANES)
        vbuf[:, 0:HALF_WIN, dst] = vp_ref[:, :, src]
        vbuf[:, HALF_WIN:HALF_WIN + tq, dst] = vc_ref[:, :, src]
        vbuf[:, HALF_WIN + tq:, dst] = vn_ref[:, :, src]
        den_lanes = (key_lane == A_HEADS + 2 * p) | (key_lane == A_HEADS + 2 * p + 1)
        vbuf[:, :, (2 * p + 1) * LANES:(2 * p + 2) * LANES] = jnp.where(den_lanes, 1.0, 0.0).astype(BF16)
    q0 = pl.program_id(2) * tq

    kj = lax.broadcasted_iota(jnp.int32, (Q_SUB, K_WIN), 1)
    qi = lax.broadcasted_iota(jnp.int32, (Q_SUB, K_WIN), 0)
    band = (kj >= qi) & (kj <= qi + 2 * HALF_WIN)
    lane = lax.broadcasted_iota(jnp.int32, (Q_SUB, LANES), 1)
    first_head = lane < A_HEAD_DIM
    odd_lane = (lane & 1) == 1
    sub_per_res = tq // Q_SUB

    def sub_block(t, carry):
        ri = t // sub_per_res
        r0 = pl.multiple_of((t % sub_per_res) * Q_SUB, Q_SUB)
        kpos = kj + (q0 + r0 - HALF_WIN)
        bias = jnp.where(band & (kpos >= 0) & (kpos < length), 0.0, NEG)
        bias = jnp.concatenate([bias, bias], axis=0)
        stat = jnp.zeros((Q_SUB, LANES), F32)
        for p in range(A_WIDTH // LANES):
            cols = slice(p * LANES, (p + 1) * LANES)
            q2 = q_ref[ri, pl.ds(r0, Q_SUB), cols]
            k2 = kbuf[ri, pl.ds(r0, K_WIN), cols]
            v2 = vbuf[ri, pl.ds(r0, K_WIN), 2 * p * LANES:(2 * p + 2) * LANES]
            zero = jnp.zeros_like(q2)
            qs = jnp.concatenate([jnp.where(first_head, q2, zero), jnp.where(first_head, zero, q2)], axis=0)
            sc = lax.dot_general(qs, k2, (((1,), (1,)), ((), ())), preferred_element_type=F32) + bias
            m = jnp.max(sc, axis=-1, keepdims=True)
            pr = jnp.exp2(sc - m)
            pvd = jnp.dot(pr.astype(BF16), v2, preferred_element_type=F32)
            pv, den = pvd[:, :LANES], pvd[:, LANES:]
            stat = stat + jnp.where(odd_lane, den[Q_SUB:], den[:Q_SUB])
            stat = jnp.where(lane == 2 * p, m[:Q_SUB], stat)
            stat = jnp.where(lane == 2 * p + 1, m[Q_SUB:], stat)
            o_ref[ri, pl.ds(r0, Q_SUB), cols] = jnp.where(first_head, pv[:Q_SUB], pv[Q_SUB:]).astype(o_ref.dtype)
        st_ref[ri, pl.ds(r0, Q_SUB), :] = stat
        return carry

    lax.fori_loop(0, nr * sub_per_res, sub_block, 0, unroll=ATT_UNROLL)


def _attention(qkv, dil):
    b, _, length, _ = qkv.shape
    tq = min(TQ_ATT, length)
    nr = min(dil, TQ_ATT // tq)
    nq = length // tq
    halo_per_q = tq // HALF_WIN
    last_halo = length // HALF_WIN - 1

    def cur(cb):
        return pl.BlockSpec((None, nr, tq, COL), lambda bi, r, i: (bi, r, i, cb))

    def prev(cb):
        return pl.BlockSpec((None, nr, HALF_WIN, COL),
                            lambda bi, r, i: (bi, r, jnp.maximum(i * halo_per_q - 1, 0), cb))

    def nxt(cb):
        return pl.BlockSpec((None, nr, HALF_WIN, COL),
                            lambda bi, r, i: (bi, r, jnp.minimum((i + 1) * halo_per_q, last_halo), cb))

    return pl.pallas_call(
        functools.partial(_attn_kernel, tq=tq, nr=nr, length=length),
        grid=(b, dil // nr, nq),
        in_specs=[cur(CB_QA), prev(CB_KA), cur(CB_KA), nxt(CB_KA), prev(CB_VA), cur(CB_VA), nxt(CB_VA)],
        out_specs=[
            pl.BlockSpec((None, nr, tq, A_WIDTH), lambda bi, r, i: (bi, r, i, 0)),
            pl.BlockSpec((None, nr, tq, LANES), lambda bi, r, i: (bi, r, i, 0)),
        ],
        out_shape=[
            jax.ShapeDtypeStruct((b, dil, length, A_WIDTH), BF16),
            jax.ShapeDtypeStruct((b, dil, length, LANES), F32),
        ],
        scratch_shapes=[pltpu.VMEM((nr, tq + 2 * HALF_WIN, COL), BF16),
                        pltpu.VMEM((nr, tq + 2 * HALF_WIN, 2 * COL), BF16)],
        compiler_params=_params(("parallel", "parallel", "parallel")),
        name=f"window_attn_d{dil}",
    )(qkv, qkv, qkv, qkv, qkv, qkv, qkv)


def _log_sigmoid(v):
    return jnp.minimum(v, 0.0) - jnp.log1p(jnp.exp(-jnp.abs(v)))


def _decays(dl_ref, head):
    lg_f = _log_sigmoid(dl_ref[pl.ds(head, 1), :])
    lg_b = _log_sigmoid(dl_ref[pl.ds(head + R_HEADS, 1), :])
    return lg_f, lg_b


def _chunk_tables(lg_f, lg_b):
    c = RET_CHUNK
    row = lax.broadcasted_iota(jnp.int32, (c, c), 0).astype(F32)
    col = lax.broadcasted_iota(jnp.int32, (c, c), 1).astype(F32)
    rel = row - col
    intra = (jnp.where(rel >= 0, jnp.exp(jnp.maximum(rel, 0.0) * lg_f), 0.0)
             + jnp.where(rel <= 0, jnp.exp(jnp.maximum(-rel, 0.0) * lg_b), 0.0))
    kd_f = jnp.exp((c - 1 - row) * lg_f)
    kd_b = jnp.exp(row * lg_b)
    qd_f = jnp.exp((row + 1.0) * lg_f)
    qd_b = jnp.exp((c - row) * lg_b)
    cd_f = jnp.exp(c * lg_f)
    cd_b = jnp.exp(c * lg_b)
    return intra, kd_f, kd_b, qd_f, qd_b, cd_f, cd_b


def _wide(v):
    return jnp.concatenate([v, v], axis=-1)


def _delta(k_chunk, kd, v_chunk):
    kt = (k_chunk.astype(F32) * kd).T.astype(BF16)
    return jnp.dot(kt, v_chunk, preferred_element_type=F32)


def _ret_state_kernel(dl_ref, kf_ref, vf_ref, kb_ref, vb_ref, fin_ref, bin_ref, sf, sb):
    head = pl.program_id(0)

    @pl.when(pl.program_id(1) == 0)
    def _():
        sf[...] = jnp.zeros_like(sf)
        sb[...] = jnp.zeros_like(sb)

    fin_ref[...] = sf[...]
    bin_ref[...] = sb[...]
    lg_f, lg_b = _decays(dl_ref, head)
    _, kd_f, kd_b, _, _, cd_f, cd_b = _chunk_tables(lg_f, lg_b)
    cd_f, cd_b = _wide(cd_f), _wide(cd_b)
    n_chunks = RET_BLOCK // RET_CHUNK

    def scan(t, carry):
        rf = pl.ds(pl.multiple_of(t * RET_CHUNK, RET_CHUNK), RET_CHUNK)
        rb = pl.ds(pl.multiple_of((n_chunks - 1 - t) * RET_CHUNK, RET_CHUNK), RET_CHUNK)
        sf[...] = sf[...] * cd_f + _delta(kf_ref[rf, :], kd_f, vf_ref[rf, :])
        sb[...] = sb[...] * cd_b + _delta(kb_ref[rb, :], kd_b, vb_ref[rb, :])
        return carry

    lax.fori_loop(0, n_chunks, scan, 0, unroll=True)


def _retention_states(p3, dl_tile):
    nb = p3.shape[0]
    k_cb = (CB_KR * COL) // R_KEY_DIM
    v_cb = OFF_VR // R_VAL_DIM
    shape = jax.ShapeDtypeStruct((nb, R_HEADS, R_KEY_DIM, R_VAL_DIM), F32)
    return pl.pallas_call(
        _ret_state_kernel,
        grid=(R_HEADS, nb),
        in_specs=[
            pl.BlockSpec((2 * R_HEADS, LANES), lambda h, j: (0, 0)),
            pl.BlockSpec((None, RET_BLOCK, R_KEY_DIM), lambda h, j: (j, 0, k_cb + h)),
            pl.BlockSpec((None, RET_BLOCK, R_VAL_DIM), lambda h, j: (j, 0, v_cb + h)),
            pl.BlockSpec((None, RET_BLOCK, R_KEY_DIM), lambda h, j: (nb - 1 - j, 0, k_cb + h)),
            pl.BlockSpec((None, RET_BLOCK, R_VAL_DIM), lambda h, j: (nb - 1 - j, 0, v_cb + h)),
        ],
        out_specs=[
            pl.BlockSpec((None, None, R_KEY_DIM, R_VAL_DIM), lambda h, j: (j, h, 0, 0)),
            pl.BlockSpec((None, None, R_KEY_DIM, R_VAL_DIM), lambda h, j: (nb - 1 - j, h, 0, 0)),
        ],
        out_shape=[shape, shape],
        scratch_shapes=[pltpu.VMEM((R_KEY_DIM, R_VAL_DIM), F32), pltpu.VMEM((R_KEY_DIM, R_VAL_DIM), F32)],
        compiler_params=_params(("parallel", "arbitrary")),
        name="retention_states",
    )(dl_tile, p3, p3, p3, p3)


def _ret_kernel(*refs, has_state):
    if has_state:
        dl_ref, gn_ref, q_ref, k_ref, v_ref, z_ref, fin_ref, bin_ref, o_ref, sf, sb, s_all = refs
    else:
        dl_ref, gn_ref, q_ref, k_ref, v_ref, z_ref, o_ref, sf, sb, s_all = refs
    head = pl.program_id(1)
    lg_f, lg_b = _decays(dl_ref, head)
    intra, kd_f, kd_b, qd_f, qd_b, cd_f, cd_b = _chunk_tables(lg_f, lg_b)
    cd_f, cd_b = _wide(cd_f), _wide(cd_b)
    gn = gn_ref[...]
    if has_state:
        sf[...] = fin_ref[...]
        sb[...] = bin_ref[...]
    else:
        sf[...] = jnp.zeros_like(sf)
        sb[...] = jnp.zeros_like(sb)
    n_chunks = RET_BLOCK // RET_CHUNK

    def scan(t, carry):
        tb = n_chunks - 1 - t
        rf = pl.ds(pl.multiple_of(t * RET_CHUNK, RET_CHUNK), RET_CHUNK)
        rb = pl.ds(pl.multiple_of(tb * RET_CHUNK, RET_CHUNK), RET_CHUNK)
        state_f = sf[...]
        s_all[t, 0:R_KEY_DIM, :] = state_f.astype(BF16)
        sf[...] = state_f * cd_f + _delta(k_ref[rf, :], kd_f, v_ref[rf, :])
        state_b = sb[...]
        s_all[tb, R_KEY_DIM:2 * R_KEY_DIM, :] = state_b.astype(BF16)
        sb[...] = state_b * cd_b + _delta(k_ref[rb, :], kd_b, v_ref[rb, :])
        return carry

    def emit(n, carry):
        rows = pl.ds(pl.multiple_of(n * RET_CHUNK, RET_CHUNK), RET_CHUNK)
        q, k, v = q_ref[rows, :], k_ref[rows, :], v_ref[rows, :]
        qf = q.astype(F32)
        qq = jnp.concatenate([(qf * qd_f).astype(BF16), (qf * qd_b).astype(BF16)], axis=-1)
        sc = lax.dot_general(q, k, (((1,), (1,)), ((), ())), preferred_element_type=F32) * intra
        y = (jnp.dot(sc.astype(BF16), v, preferred_element_type=F32)
             + jnp.dot(qq, s_all[n], preferred_element_type=F32))
        mu = jnp.mean(y, axis=-1, keepdims=True)
        yc = y - mu
        var = jnp.mean(yc * yc, axis=-1, keepdims=True)
        yn = yc * lax.rsqrt(var + EPS) * gn
        o_ref[rows, :] = _times_silu_of_half(yn, z_ref[rows, :].astype(F32)).astype(BF16)
        return carry

    lax.fori_loop(0, n_chunks, scan, 0, unroll=True)
    lax.fori_loop(0, n_chunks, emit, 0, unroll=True)


def _retention(p3, dl_tile, gn, states):
    nb = p3.shape[0]
    q_cb = (CB_QR * COL) // R_KEY_DIM
    k_cb = (CB_KR * COL) // R_KEY_DIM
    v_cb = OFF_VR // R_VAL_DIM
    z_cb = OFF_ZR // R_VAL_DIM
    in_specs = [
        pl.BlockSpec((2 * R_HEADS, LANES), lambda b, h: (0, 0)),
        pl.BlockSpec((1, R_VAL_DIM), lambda b, h: (0, h)),
        pl.BlockSpec((None, RET_BLOCK, R_KEY_DIM), lambda b, h: (b, 0, q_cb + h)),
        pl.BlockSpec((None, RET_BLOCK, R_KEY_DIM), lambda b, h: (b, 0, k_cb + h)),
        pl.BlockSpec((None, RET_BLOCK, R_VAL_DIM), lambda b, h: (b, 0, v_cb + h)),
        pl.BlockSpec((None, RET_BLOCK, R_VAL_DIM), lambda b, h: (b, 0, z_cb + h)),
    ]
    args = [dl_tile, gn, p3, p3, p3, p3]
    if states is not None:
        st = pl.BlockSpec((None, None, R_KEY_DIM, R_VAL_DIM), lambda b, h: (b, h, 0, 0))
        in_specs += [st, st]
        args += list(states)
    n_chunks = RET_BLOCK // RET_CHUNK
    out = pl.pallas_call(
        functools.partial(_ret_kernel, has_state=states is not None),
        grid=(nb, R_HEADS),
        in_specs=in_specs,
        out_specs=pl.BlockSpec((None, RET_BLOCK, R_VAL_DIM), lambda b, h: (b, 0, h)),
        out_shape=jax.ShapeDtypeStruct((nb, RET_BLOCK, R_V_WIDTH), BF16),
        scratch_shapes=[
            pltpu.VMEM((R_KEY_DIM, R_VAL_DIM), F32),
            pltpu.VMEM((R_KEY_DIM, R_VAL_DIM), F32),
            pltpu.VMEM((n_chunks, 2 * R_KEY_DIM, R_VAL_DIM), BF16),
        ],
        compiler_params=_params(("parallel", "parallel")),
        name="retention",
    )(*args)
    return out.reshape(nb * RET_BLOCK, R_V_WIDTH)


def _out_kernel(x_ref, mod_ref, o1_ref, o4_ref, o16_ref, l1_ref, l4_ref, l16_ref, za_ref, ur_ref, ga_ref, gr_ref,
                wa_ref, wb_ref, wo_ref, y_ref, o_scr, l_scr):
    tm = x_ref.shape[0]
    for gi, (dil, o_ref, l_ref) in enumerate(((4, o4_ref, l4_ref), (16, o16_ref, l16_ref))):
        n = tm // dil
        for r in range(dil):
            l_scr[gi, pl.ds(r, n, stride=dil), :] = l_ref[r]
            for c in range(N_CHUNK):
                o_scr[gi, c, pl.ds(r, n, stride=dil), :] = o_ref[r, :, c * LANES:(c + 1) * LANES].astype(F32)

    lane = lax.broadcasted_iota(jnp.int32, (OUT_SUB, LANES), 1)
    first_head = lane < A_HEAD_DIM
    for s in range(tm // OUT_SUB):
        rows = slice(s * OUT_SUB, (s + 1) * OUT_SUB)
        stats = (l1_ref[rows, :], l_scr[0, rows, :], l_scr[1, rows, :])
        top = jnp.maximum(jnp.maximum(stats[0], stats[1]), stats[2])
        es = [jnp.exp2(v - top) for v in stats]
        dens = [pltpu.roll(v, LANES - A_HEADS, 1) for v in stats]
        total = es[0] * dens[0] + es[1] * dens[1] + es[2] * dens[2]
        total = jnp.where(lane < A_HEADS, total, 1.0)
        ws = [e / total for e in es]
        ua = []
        for c in range(N_CHUNK):
            cols = slice(c * LANES, (c + 1) * LANES)
            outs = (o1_ref[rows, cols].astype(F32), o_scr[0, c, rows, :], o_scr[1, c, rows, :])
            ya = jnp.zeros((OUT_SUB, LANES), F32)
            for g in range(3):
                w_pair = jnp.where(first_head, ws[g][:, 2 * c:2 * c + 1], ws[g][:, 2 * c + 1:2 * c + 2])
                ya = ya + w_pair * outs[g]
            ua.append(_times_silu_of_half(ya, za_ref[rows, cols].astype(F32)).astype(BF16))
        ua = jnp.concatenate(ua, axis=-1)
        ya_p = jnp.dot(ua, wa_ref[...], preferred_element_type=F32)
        yr_p = jnp.dot(ur_ref[rows, :], wb_ref[...], preferred_element_type=F32)
        ta = jnp.tanh(ga_ref[rows, :].astype(F32))
        tr = jnp.tanh(gr_ref[rows, :].astype(F32))
        merged2 = (ya_p + ya_p * ta) + (yr_p + yr_p * tr)
        out = jnp.dot(merged2.astype(BF16), wo_ref[...], preferred_element_type=F32)
        y_ref[rows, :] = x_ref[rows, :] + (0.5 * mod_ref[2:3, :]) * out


def _output(x2, mod3, layer, b_off, seq, attn, p2, ur, wa, wb, wo):
    t = x2.shape[0]
    per_seq = seq // TM_OUT
    const = dict(pipeline_mode=pl.Buffered(1))
    (o1, l1), (o4, l4), (o16, l16) = attn

    def rows(width, cb=0):
        return pl.BlockSpec((TM_OUT, width), lambda i: (i, cb))

    def dilated(dil, width):
        return pl.BlockSpec((None, dil, TM_OUT // dil, width), lambda i: (i // per_seq, 0, i % per_seq, 0))

    return pl.pallas_call(
        _out_kernel,
        grid=(t // TM_OUT,),
        in_specs=[
            rows(D_MODEL),
            pl.BlockSpec((None, None, 3, D_MODEL), lambda i: (layer, i // per_seq + b_off, 0, 0)),
            rows(A_WIDTH), dilated(4, A_WIDTH), dilated(16, A_WIDTH),
            rows(LANES), dilated(4, LANES), dilated(16, LANES),
            rows(A_WIDTH, CB_ZA),
            rows(R_V_WIDTH),
            rows(D_MODEL, OFF_GA // D_MODEL),
            rows(D_MODEL, OFF_GR // D_MODEL),
            pl.BlockSpec((A_WIDTH, D_MODEL), lambda i: (0, 0), **const),
            pl.BlockSpec((R_V_WIDTH, D_MODEL), lambda i: (0, 0), **const),
            pl.BlockSpec((D_MODEL, D_MODEL), lambda i: (0, 0), **const),
        ],
        out_specs=rows(D_MODEL),
        out_shape=jax.ShapeDtypeStruct((t, D_MODEL), F32),
        scratch_shapes=[pltpu.VMEM((2, N_CHUNK, TM_OUT, LANES), F32), pltpu.VMEM((2, TM_OUT, LANES), F32)],
        compiler_params=_params(("parallel",)),
        name="merge_out_proj",
    )(x2, mod3, o1.reshape(t, A_WIDTH), o4, o16, l1.reshape(t, LANES), l4, l16, p2, ur, p2, p2, wa, wb, wo)


def _mixer_layer(x2, batch, seq, mod3, layer, b_off, lw, tables, e_bd):
    p2, a4, a16 = _inproj(x2, batch, seq, mod3, layer, b_off, lw["norm_g"], lw["w_in"], lw["gq"], lw["gk"], e_bd,
                          tables)
    attn = [_attention(qkv, dil) for qkv, dil in zip((p2.reshape(batch, 1, seq, IN_WIDTH), a4, a16), DILATIONS)]
    nb = batch * seq // RET_BLOCK
    pr = p2.reshape(nb, RET_BLOCK, IN_WIDTH)
    states = _retention_states(pr, lw["dl"]) if seq > RET_BLOCK else None
    ur = _retention(pr, lw["dl"], lw["gn"], states)
    return _output(x2, mod3, layer, b_off, seq, attn, p2, ur, lw["wa"], lw["wb"], lw["wo"])


def _half_gate_columns():
    col = jnp.arange(IN_WIDTH)
    za = (col >= CB_ZA * COL) & (col < (CB_ZA + 1) * COL)
    return jnp.where(za | (col >= OFF_ZR), 0.5, 1.0).astype(F32)[None, :]


def _layer_weights(layer,norm_g, w_in, q_norm_g, k_norm_g, ret_decay_logit, ret_norm_g, w_proj_a, w_proj_b, w_out):
    return {
        "norm_g": norm_g[layer].reshape(1, D_MODEL),
        "w_in": (w_in[layer] * _half_gate_columns()).astype(BF16),
        "gq": (jnp.tile(q_norm_g[layer], A_HEADS) * (A_HEAD_DIM ** -0.5 * LOG2E)).reshape(1, A_WIDTH),
        "gk": jnp.tile(k_norm_g[layer], A_HEADS).reshape(1, A_WIDTH),
        "dl": jnp.broadcast_to(ret_decay_logit[layer].astype(F32).reshape(2 * R_HEADS, 1), (2 * R_HEADS, LANES)),
        "gn": ret_norm_g[layer].reshape(1, R_V_WIDTH),
        "wa": w_proj_a[layer].astype(BF16),
        "wb": w_proj_b[layer].astype(BF16),
        "wo": w_out[layer].astype(BF16),
    }


def kernel(x_prompt, x_sample, c_prompt, c_sample, norm_g, w_ada, b_ada, w_in, q_norm_g, k_norm_g, ret_decay_logit,
           ret_norm_g, w_proj_a, w_proj_b, w_out):
    bp, sp, _ = x_prompt.shape
    bs, ss, _ = x_sample.shape
    pad = (-(bp + bs)) % 8
    c_all = jnp.concatenate([c_prompt, c_sample, jnp.zeros((pad, D_MODEL), F32)], axis=0)
    mod3 = _modulation(c_all, w_ada, b_ada).reshape(DEPTH, bp + bs + pad, 3, D_MODEL)
    head_of_lane = jnp.arange(E_BLK) // A_HEAD_DIM
    e_bd = (head_of_lane[:, None] == head_of_lane[None, :]).astype(BF16)
    tab_p = _rope_tables(sp)
    tab_s = tab_p if ss == sp else _rope_tables(ss)
    yp = x_prompt.reshape(bp * sp, D_MODEL)
    ys = x_sample.reshape(bs * ss, D_MODEL)
    for layer in range(DEPTH):
        lw = _layer_weights(layer, norm_g, w_in, q_norm_g, k_norm_g, ret_decay_logit, ret_norm_g, w_proj_a, w_proj_b,
                            w_out)
        yp = _mixer_layer(yp, bp, sp, mod3, layer, 0, lw, tab_p, e_bd)
        ys = _mixer_layer(ys, bs, ss, mod3, layer, bp, lw, tab_s, e_bd)
    return yp.reshape(bp, sp, D_MODEL), ys.reshape(bs, ss, D_MODEL)
```

```python
import functools
import math

import jax
import jax.numpy as jnp
from jax import lax
from jax.experimental import pallas as pl
from jax.experimental.pallas import tpu as pltpu

D_MODEL = 1024
DEPTH = 4
A_HEADS = 8
A_HEAD_DIM = 64
A_WIDTH = A_HEADS * A_HEAD_DIM
A_ROT_HALF = A_HEAD_DIM // 8
ATT_THETA = 500000.0
DILATIONS = (1, 4, 16)
HALF_WIN = 64
R_HEADS = 4
R_KEY_DIM = 128
R_VAL_DIM = 256
R_V_WIDTH = R_HEADS * R_VAL_DIM
RET_THETA = 10000.0
RET_CHUNK = 128
IN_WIDTH = 4 * A_WIDTH + 2 * R_HEADS * R_KEY_DIM + 2 * R_V_WIDTH + 2 * D_MODEL
EPS = 1e-6
NEG = -1e30
LOG2E = 1.4426950408889634

LANES = 128
COL = 512
N_COL = IN_WIDTH // COL
N_CHUNK = COL // LANES
E_BLK = 256
CB_QA, CB_KA, CB_VA, CB_ZA, CB_QR, CB_KR = 0, 1, 2, 3, 4, 5
OFF_VR, OFF_ZR, OFF_GA, OFF_GR = 3072, 4096, 5120, 6144
QKV_W = 3 * A_WIDTH
ROPE_TABLE_W = 5 * LANES

TM_IN = 512
TM_OUT = 1024
OUT_SUB = 256
TQ_ATT = 2048
Q_SUB = 128
K_WIN = Q_SUB + 2 * HALF_WIN
ATT_UNROLL = 16
RET_BLOCK = 2048
RET_HEADS_PER_STEP = 2
VMEM_LIMIT = 56 * 1024 * 1024

F32 = jnp.float32
BF16 = jnp.bfloat16


def _silu(v):
    h = 0.5 * v
    return h + h * jnp.tanh(h)


def _times_silu_of_half(u, h):
    uh = u * h
    return uh + uh * jnp.tanh(h)


def _params(sem, vmem=VMEM_LIMIT):
    return pltpu.CompilerParams(dimension_semantics=sem, vmem_limit_bytes=vmem)


def _mod_kernel(c_ref, w_ref, b_ref, o_ref):
    s = _silu(c_ref[...])
    o_ref[...] = jnp.dot(s, w_ref[...], preferred_element_type=F32, precision=lax.Precision.HIGHEST) + b_ref[...]


def _modulation(c_all, w_ada, b_ada):
    nb = c_all.shape[0]
    return pl.pallas_call(
        _mod_kernel,
        grid=(DEPTH, 3),
        in_specs=[
            pl.BlockSpec((nb, D_MODEL), lambda l, j: (0, 0)),
            pl.BlockSpec((None, D_MODEL, D_MODEL), lambda l, j: (l, 0, j)),
            pl.BlockSpec((None, 1, D_MODEL), lambda l, j: (l, 0, j)),
        ],
        out_specs=pl.BlockSpec((None, nb, D_MODEL), lambda l, j: (l, 0, j)),
        out_shape=jax.ShapeDtypeStruct((DEPTH, nb, 3 * D_MODEL), F32),
        compiler_params=_params(("parallel", "parallel")),
        name="adaln_mod",
    )(c_all, w_ada, b_ada.reshape(DEPTH, 1, 3 * D_MODEL))


def _rope_tables(seq):
    pos = jnp.arange(seq, dtype=F32)
    fa = jnp.exp(-math.log(ATT_THETA) * jnp.arange(A_ROT_HALF, dtype=F32) / A_ROT_HALF)
    ang = pos[:, None] * fa[None, :]
    cos, sin = jnp.cos(ang), jnp.sin(ang)
    rest = A_HEAD_DIM - 2 * A_ROT_HALF
    one = jnp.ones((seq, rest), F32)
    zero = jnp.zeros((seq, rest), F32)
    z8 = jnp.zeros((seq, A_ROT_HALF), F32)
    att_c = jnp.tile(jnp.concatenate([cos, cos, one], 1), (1, 2))
    att_a = jnp.tile(jnp.concatenate([-sin, z8, zero], 1), (1, 2))
    att_b = jnp.tile(jnp.concatenate([z8, sin, zero], 1), (1, 2))
    half = R_KEY_DIM // 2
    fr = jnp.exp(-math.log(RET_THETA) * jnp.arange(half, dtype=F32) / half)
    ang = pos[:, None] * fr[None, :]
    cos, sin = jnp.cos(ang), jnp.sin(ang)
    ret_c = jnp.concatenate([cos, cos], 1)
    ret_s = jnp.concatenate([-sin, sin], 1)
    return jnp.concatenate([att_c, att_a, att_b, ret_c, ret_s], 1)


def _inproj_kernel(x_ref, mod_ref, g_ref, w_ref, gq_ref, gk_ref, e_ref, tab_ref, o_ref, a4_ref, a16_ref, rows_scr):
    x = x_ref[...]
    ms = jnp.mean(x * x, axis=-1, keepdims=True)
    y = x * lax.rsqrt(ms + EPS) * g_ref[...]
    h = (y * (1.0 + mod_ref[1:2, :]) + mod_ref[0:1, :]).astype(BF16)

    att_c = tab_ref[:, 0 * LANES:1 * LANES]
    att_a = tab_ref[:, 1 * LANES:2 * LANES]
    att_b = tab_ref[:, 2 * LANES:3 * LANES]
    ret_c = tab_ref[:, 3 * LANES:4 * LANES]
    ret_s = tab_ref[:, 4 * LANES:5 * LANES]

    def emit_qkv(j, c, val):
        lo = j * COL + c * LANES
        o_ref[:, lo:lo + LANES] = val.astype(BF16)
        rows_scr[0, c] = val
        n4 = TM_IN // 4
        n16 = TM_IN // 16
        for r4 in range(4):
            by4 = rows_scr[0, c, pl.ds(r4, n4, stride=4), :]
            a4_ref[r4, :, lo:lo + LANES] = by4.astype(BF16)
            rows_scr[1, c, r4 * n4:(r4 + 1) * n4, :] = by4
        for r4 in range(4):
            for r in range(4):
                by16 = rows_scr[1, c, pl.ds(r4 * n4 + r, n16, stride=4), :]
                a16_ref[4 * r + r4, :, lo:lo + LANES] = by16.astype(BF16)

    for j in range(N_COL):
        acc = jnp.dot(h, w_ref[:, j * COL:(j + 1) * COL], preferred_element_type=F32)
        if j in (CB_QA, CB_KA):
            sq = (acc * acc).astype(BF16)
            ss = jnp.concatenate(
                [jnp.dot(sq[:, s * E_BLK:(s + 1) * E_BLK], e_ref[...], preferred_element_type=F32)
                 for s in range(COL // E_BLK)], axis=-1)
            gain = gq_ref[...] if j == CB_QA else gk_ref[...]
            yn = acc * lax.rsqrt(ss * (1.0 / A_HEAD_DIM) + EPS) * gain
            for c in range(N_CHUNK):
                xc = yn[:, c * LANES:(c + 1) * LANES]
                rot = (xc * att_c + pltpu.roll(xc, LANES - A_ROT_HALF, 1) * att_a
                       + pltpu.roll(xc, A_ROT_HALF, 1) * att_b)
                emit_qkv(j, c, rot)
        elif j == CB_VA:
            for c in range(N_CHUNK):
                emit_qkv(j, c, acc[:, c * LANES:(c + 1) * LANES])
        elif j in (CB_QR, CB_KR):
            for c in range(N_CHUNK):
                xc = acc[:, c * LANES:(c + 1) * LANES]
                rot = xc * ret_c + pltpu.roll(xc, R_KEY_DIM // 2, 1) * ret_s
                if j == CB_KR:
                    rot = rot * (R_KEY_DIM ** -0.5)
                o_ref[:, j * COL + c * LANES:j * COL + (c + 1) * LANES] = rot.astype(BF16)
        else:
            o_ref[:, j * COL:(j + 1) * COL] = acc.astype(BF16)


def _inproj(x2, batch, seq, mod3, layer, b_off, norm_g, w_in, gq, gk, e_bd, tables):
    t = x2.shape[0]
    per_seq = seq // TM_IN
    const = dict(pipeline_mode=pl.Buffered(1))

    def dilated(dil):
        return pl.BlockSpec((None, dil, TM_IN // dil, QKV_W), lambda i: (i // per_seq, 0, i % per_seq, 0))

    return pl.pallas_call(
        _inproj_kernel,
        grid=(t // TM_IN,),
        in_specs=[
            pl.BlockSpec((TM_IN, D_MODEL), lambda i: (i, 0)),
            pl.BlockSpec((None, None, 3, D_MODEL), lambda i: (layer, i // per_seq + b_off, 0, 0)),
            pl.BlockSpec((1, D_MODEL), lambda i: (0, 0)),
            pl.BlockSpec((D_MODEL, IN_WIDTH), lambda i: (0, 0), **const),
            pl.BlockSpec((1, COL), lambda i: (0, 0)),
            pl.BlockSpec((1, COL), lambda i: (0, 0)),
            pl.BlockSpec((E_BLK, E_BLK), lambda i: (0, 0), **const),
            pl.BlockSpec((TM_IN, ROPE_TABLE_W), lambda i: (i % per_seq, 0)),
        ],
        out_specs=[pl.BlockSpec((TM_IN, IN_WIDTH), lambda i: (i, 0)), dilated(4), dilated(16)],
        out_shape=[
            jax.ShapeDtypeStruct((t, IN_WIDTH), BF16),
            jax.ShapeDtypeStruct((batch, 4, seq // 4, QKV_W), BF16),
            jax.ShapeDtypeStruct((batch, 16, seq // 16, QKV_W), BF16),
        ],
        scratch_shapes=[pltpu.VMEM((2, N_CHUNK, TM_IN, LANES), F32)],
        compiler_params=_params(("parallel",)),
        name="in_proj",
    )(x2, mod3, norm_g, w_in, gq, gk, e_bd, tables)


def _attn_kernel(q_ref, kp_ref, kc_ref, kn_ref, vp_ref, vc_ref, vn_ref, o_ref, st_ref, kbuf, vbuf, *, tq, nr, length):
    kbuf[:, 0:HALF_WIN, :] = kp_ref[...]
    kbuf[:, HALF_WIN:HALF_WIN + tq, :] = kc_ref[...]
    kbuf[:, HALF_WIN + tq:, :] = kn_ref[...]
    key_lane = lax.broadcasted_iota(jnp.int32, (nr, tq + 2 * HALF_WIN, LANES), 2)
    for p in range(A_WIDTH // LANES):
        src = slice(p * LANES, (p + 1) * LANES)
        dst = slice(2 * p * LANES, (2 * p + 1) * LANES)
        vbuf[:, 0:HALF_WIN, dst] = vp_ref[:, :, src]
        vbuf[:, HALF_WIN:HALF_WIN + tq, dst] = vc_ref[:, :, src]
        vbuf[:, HALF_WIN + tq:, dst] = vn_ref[:, :, src]
        den_lanes = (key_lane == A_HEADS + 2 * p) | (key_lane == A_HEADS + 2 * p + 1)
        vbuf[:, :, (2 * p + 1) * LANES:(2 * p + 2) * LANES] = jnp.where(den_lanes, 1.0, 0.0).astype(BF16)
    q0 = pl.program_id(2) * tq

    kj = lax.broadcasted_iota(jnp.int32, (Q_SUB, K_WIN), 1)
    qi = lax.broadcasted_iota(jnp.int32, (Q_SUB, K_WIN), 0)
    band = (kj >= qi) & (kj <= qi + 2 * HALF_WIN)
    lane = lax.broadcasted_iota(jnp.int32, (Q_SUB, LANES), 1)
    first_head = lane < A_HEAD_DIM
    odd_lane = (lane & 1) == 1
    sub_per_res = tq // Q_SUB

    def sub_block(t, carry):
        ri = t // sub_per_res
        r0 = pl.multiple_of((t % sub_per_res) * Q_SUB, Q_SUB)
        kpos = kj + (q0 + r0 - HALF_WIN)
        bias = jnp.where(band & (kpos >= 0) & (kpos < length), 0.0, NEG)
        bias = jnp.concatenate([bias, bias], axis=0)
        stat = jnp.zeros((Q_SUB, LANES), F32)
        for p in range(A_WIDTH // LANES):
            cols = slice(p * LANES, (p + 1) * LANES)
            q2 = q_ref[ri, pl.ds(r0, Q_SUB), cols]
            k2 = kbuf[ri, pl.ds(r0, K_WIN), cols]
            v2 = vbuf[ri, pl.ds(r0, K_WIN), 2 * p * LANES:(2 * p + 2) * LANES]
            zero = jnp.zeros_like(q2)
            qs = jnp.concatenate([jnp.where(first_head, q2, zero), jnp.where(first_head, zero, q2)], axis=0)
            sc = lax.dot_general(qs, k2, (((1,), (1,)), ((), ())), preferred_element_type=F32) + bias
            m = jnp.max(sc, axis=-1, keepdims=True)
            pr = jnp.exp2(sc - m)
            pvd = jnp.dot(pr.astype(BF16), v2, preferred_element_type=F32)
            pv, den = pvd[:, :LANES], pvd[:, LANES:]
            stat = stat + jnp.where(odd_lane, den[Q_SUB:], den[:Q_SUB])
            stat = jnp.where(lane == 2 * p, m[:Q_SUB], stat)
            stat = jnp.where(lane == 2 * p + 1, m[Q_SUB:], stat)
            o_ref[ri, pl.ds(r0, Q_SUB), cols] = jnp.where(first_head, pv[:Q_SUB], pv[Q_SUB:]).astype(o_ref.dtype)
        st_ref[ri, pl.ds(r0, Q_SUB), :] = stat
        return carry

    lax.fori_loop(0, nr * sub_per_res, sub_block, 0, unroll=ATT_UNROLL)


def _attention(qkv, dil):
    b, _, length, _ = qkv.shape
    tq = min(TQ_ATT, length)
    nr = min(dil, TQ_ATT // tq)
    nq = length // tq
    halo_per_q = tq // HALF_WIN
    last_halo = length // HALF_WIN - 1

    def cur(cb):
        return pl.BlockSpec((None, nr, tq, COL), lambda bi, r, i: (bi, r, i, cb))

    def prev(cb):
        return pl.BlockSpec((None, nr, HALF_WIN, COL),
                            lambda bi, r, i: (bi, r, jnp.maximum(i * halo_per_q - 1, 0), cb))

    def nxt(cb):
        return pl.BlockSpec((None, nr, HALF_WIN, COL),
                            lambda bi, r, i: (bi, r, jnp.minimum((i + 1) * halo_per_q, last_halo), cb))

    return pl.pallas_call(
        functools.partial(_attn_kernel, tq=tq, nr=nr, length=length),
        grid=(b, dil // nr, nq),
        in_specs=[cur(CB_QA), prev(CB_KA), cur(CB_KA), nxt(CB_KA), prev(CB_VA), cur(CB_VA), nxt(CB_VA)],
        out_specs=[
            pl.BlockSpec((None, nr, tq, A_WIDTH), lambda bi, r, i: (bi, r, i, 0)),
            pl.BlockSpec((None, nr, tq, LANES), lambda bi, r, i: (bi, r, i, 0)),
        ],
        out_shape=[
            jax.ShapeDtypeStruct((b, dil, length, A_WIDTH), BF16),
            jax.ShapeDtypeStruct((b, dil, length, LANES), F32),
        ],
        scratch_shapes=[pltpu.VMEM((nr, tq + 2 * HALF_WIN, COL), BF16),
                        pltpu.VMEM((nr, tq + 2 * HALF_WIN, 2 * COL), BF16)],
        compiler_params=_params(("parallel", "parallel", "parallel")),
        name=f"window_attn_d{dil}",
    )(qkv, qkv, qkv, qkv, qkv, qkv, qkv)


def _log_sigmoid(v):
    return jnp.minimum(v, 0.0) - jnp.log1p(jnp.exp(-jnp.abs(v)))


def _decays(dl_ref, head):
    lg_f = _log_sigmoid(dl_ref[pl.ds(head, 1), :])
    lg_b = _log_sigmoid(dl_ref[pl.ds(head + R_HEADS, 1), :])
    return lg_f, lg_b


def _chunk_tables(lg_f, lg_b):
    c = RET_CHUNK
    row = lax.broadcasted_iota(jnp.int32, (c, c), 0).astype(F32)
    col = lax.broadcasted_iota(jnp.int32, (c, c), 1).astype(F32)
    rel = row - col
    intra = (jnp.where(rel >= 0, jnp.exp(jnp.maximum(rel, 0.0) * lg_f), 0.0)
             + jnp.where(rel <= 0, jnp.exp(jnp.maximum(-rel, 0.0) * lg_b), 0.0))
    kd_f = jnp.exp((c - 1 - row) * lg_f)
    kd_b = jnp.exp(row * lg_b)
    qd_f = jnp.exp((row + 1.0) * lg_f)
    qd_b = jnp.exp((c - row) * lg_b)
    cd_f = jnp.exp(c * lg_f)
    cd_b = jnp.exp(c * lg_b)
    return intra, kd_f, kd_b, qd_f, qd_b, cd_f, cd_b


def _wide(v):
    return jnp.concatenate([v, v], axis=-1)


def _delta(k_chunk, kd, v_chunk):
    kt = (k_chunk.astype(F32) * kd).T.astype(BF16)
    return jnp.dot(kt, v_chunk, preferred_element_type=F32)


def _ret_state_kernel(dl_ref, kf_ref, vf_ref, kb_ref, vb_ref, fin_ref, bin_ref, sf, sb):
    head = pl.program_id(0)

    @pl.when(pl.program_id(1) == 0)
    def _():
        sf[...] = jnp.zeros_like(sf)
        sb[...] = jnp.zeros_like(sb)

    fin_ref[...] = sf[...]
    bin_ref[...] = sb[...]
    lg_f, lg_b = _decays(dl_ref, head)
    _, kd_f, kd_b, _, _, cd_f, cd_b = _chunk_tables(lg_f, lg_b)
    cd_f, cd_b = _wide(cd_f), _wide(cd_b)
    n_chunks = RET_BLOCK // RET_CHUNK

    def scan(t, carry):
        rf = pl.ds(pl.multiple_of(t * RET_CHUNK, RET_CHUNK), RET_CHUNK)
        rb = pl.ds(pl.multiple_of((n_chunks - 1 - t) * RET_CHUNK, RET_CHUNK), RET_CHUNK)
        sf[...] = sf[...] * cd_f + _delta(kf_ref[rf, :], kd_f, vf_ref[rf, :])
        sb[...] = sb[...] * cd_b + _delta(kb_ref[rb, :], kd_b, vb_ref[rb, :])
        return carry

    lax.fori_loop(0, n_chunks, scan, 0, unroll=True)


def _retention_states(p3, dl_tile):
    nb = p3.shape[0]
    k_cb = (CB_KR * COL) // R_KEY_DIM
    v_cb = OFF_VR // R_VAL_DIM
    shape = jax.ShapeDtypeStruct((nb, R_HEADS, R_KEY_DIM, R_VAL_DIM), F32)
    return pl.pallas_call(
        _ret_state_kernel,
        grid=(R_HEADS, nb),
        in_specs=[
            pl.BlockSpec((2 * R_HEADS, LANES), lambda h, j: (0, 0)),
            pl.BlockSpec((None, RET_BLOCK, R_KEY_DIM), lambda h, j: (j, 0, k_cb + h)),
            pl.BlockSpec((None, RET_BLOCK, R_VAL_DIM), lambda h, j: (j, 0, v_cb + h)),
            pl.BlockSpec((None, RET_BLOCK, R_KEY_DIM), lambda h, j: (nb - 1 - j, 0, k_cb + h)),
            pl.BlockSpec((None, RET_BLOCK, R_VAL_DIM), lambda h, j: (nb - 1 - j, 0, v_cb + h)),
        ],
        out_specs=[
            pl.BlockSpec((None, None, R_KEY_DIM, R_VAL_DIM), lambda h, j: (j, h, 0, 0)),
            pl.BlockSpec((None, None, R_KEY_DIM, R_VAL_DIM), lambda h, j: (nb - 1 - j, h, 0, 0)),
        ],
        out_shape=[shape, shape],
        scratch_shapes=[pltpu.VMEM((R_KEY_DIM, R_VAL_DIM), F32), pltpu.VMEM((R_KEY_DIM, R_VAL_DIM), F32)],
        compiler_params=_params(("parallel", "arbitrary")),
        name="retention_states",
    )(dl_tile, p3, p3, p3, p3)


def _ret_kernel(*refs, has_state):
    if has_state:
        dl_ref, gn_ref, q_ref, k_ref, v_ref, z_ref, fin_ref, bin_ref, o_ref, sf, sb, s_all = refs
    else:
        dl_ref, gn_ref, q_ref, k_ref, v_ref, z_ref, o_ref, sf, sb, s_all = refs
    n_chunks = RET_BLOCK // RET_CHUNK

    for hh in range(RET_HEADS_PER_STEP):
        head = pl.program_id(1) * RET_HEADS_PER_STEP + hh
        kcols = slice(hh * R_KEY_DIM, (hh + 1) * R_KEY_DIM)
        vcols = slice(hh * R_VAL_DIM, (hh + 1) * R_VAL_DIM)
        lg_f, lg_b = _decays(dl_ref, head)
        intra, kd_f, kd_b, qd_f, qd_b, cd_f, cd_b = _chunk_tables(lg_f, lg_b)
        cd_f, cd_b = _wide(cd_f), _wide(cd_b)
        gn = gn_ref[:, vcols]
        if has_state:
            sf[hh] = fin_ref[hh]
            sb[hh] = bin_ref[hh]
        else:
            sf[hh] = jnp.zeros((R_KEY_DIM, R_VAL_DIM), F32)
            sb[hh] = jnp.zeros((R_KEY_DIM, R_VAL_DIM), F32)

        for t in range(n_chunks):
            tb = n_chunks - 1 - t
            rf = slice(t * RET_CHUNK, (t + 1) * RET_CHUNK)
            rb = slice(tb * RET_CHUNK, (tb + 1) * RET_CHUNK)
            state_f = sf[hh]
            s_all[hh, t, 0:R_KEY_DIM, :] = state_f.astype(BF16)
            sf[hh] = state_f * cd_f + _delta(k_ref[rf, kcols], kd_f, v_ref[rf, vcols])
            state_b = sb[hh]
            s_all[hh, tb, R_KEY_DIM:2 * R_KEY_DIM, :] = state_b.astype(BF16)
            sb[hh] = state_b * cd_b + _delta(k_ref[rb, kcols], kd_b, v_ref[rb, vcols])

        for n in range(n_chunks):
            rows = slice(n * RET_CHUNK, (n + 1) * RET_CHUNK)
            q, k, v = q_ref[rows, kcols], k_ref[rows, kcols], v_ref[rows, vcols]
            qf = q.astype(F32)
            qq = jnp.concatenate([(qf * qd_f).astype(BF16), (qf * qd_b).astype(BF16)], axis=-1)
            sc = lax.dot_general(q, k, (((1,), (1,)), ((), ())), preferred_element_type=F32) * intra
            y = (jnp.dot(sc.astype(BF16), v, preferred_element_type=F32)
                 + jnp.dot(qq, s_all[hh, n], preferred_element_type=F32))
            mu = jnp.mean(y, axis=-1, keepdims=True)
            yc = y - mu
            var = jnp.mean(yc * yc, axis=-1, keepdims=True)
            yn = yc * lax.rsqrt(var + EPS) * gn
            o_ref[rows, vcols] = _times_silu_of_half(yn, z_ref[rows, vcols].astype(F32)).astype(BF16)


def _retention(p3, dl_tile, gn, states):
    nb = p3.shape[0]
    hps = RET_HEADS_PER_STEP
    kw, vw = hps * R_KEY_DIM, hps * R_VAL_DIM
    q_cb = (CB_QR * COL) // kw
    k_cb = (CB_KR * COL) // kw
    v_cb = OFF_VR // vw
    z_cb = OFF_ZR // vw
    in_specs = [
        pl.BlockSpec((2 * R_HEADS, LANES), lambda b, h: (0, 0)),
        pl.BlockSpec((1, vw), lambda b, h: (0, h)),
        pl.BlockSpec((None, RET_BLOCK, kw), lambda b, h: (b, 0, q_cb + h)),
        pl.BlockSpec((None, RET_BLOCK, kw), lambda b, h: (b, 0, k_cb + h)),
        pl.BlockSpec((None, RET_BLOCK, vw), lambda b, h: (b, 0, v_cb + h)),
        pl.BlockSpec((None, RET_BLOCK, vw), lambda b, h: (b, 0, z_cb + h)),
    ]
    args = [dl_tile, gn, p3, p3, p3, p3]
    if states is not None:
        st = pl.BlockSpec((None, hps, R_KEY_DIM, R_VAL_DIM), lambda b, h: (b, h, 0, 0))
        in_specs += [st, st]
        args += list(states)
    n_chunks = RET_BLOCK // RET_CHUNK
    out = pl.pallas_call(
        functools.partial(_ret_kernel, has_state=states is not None),
        grid=(nb, R_HEADS // hps),
        in_specs=in_specs,
        out_specs=pl.BlockSpec((None, RET_BLOCK, vw), lambda b, h: (b, 0, h)),
        out_shape=jax.ShapeDtypeStruct((nb, RET_BLOCK, R_V_WIDTH), BF16),
        scratch_shapes=[
            pltpu.VMEM((hps, R_KEY_DIM, R_VAL_DIM), F32),
            pltpu.VMEM((hps, R_KEY_DIM, R_VAL_DIM), F32),
            pltpu.VMEM((hps, n_chunks, 2 * R_KEY_DIM, R_VAL_DIM), BF16),
        ],
        compiler_params=_params(("parallel", "parallel")),
        name="retention",
    )(*args)
    return out.reshape(nb * RET_BLOCK, R_V_WIDTH)


def _out_kernel(x_ref, mod_ref, o1_ref, o4_ref, o16_ref, l1_ref, l4_ref, l16_ref, za_ref, ur_ref, ga_ref, gr_ref,
                wa_ref, wb_ref, wo_ref, y_ref, o_scr, l_scr):
    tm = x_ref.shape[0]
    for gi, (dil, o_ref, l_ref) in enumerate(((4, o4_ref, l4_ref), (16, o16_ref, l16_ref))):
        n = tm // dil
        for r in range(dil):
            l_scr[gi, pl.ds(r, n, stride=dil), :] = l_ref[r]
            for c in range(N_CHUNK):
                o_scr[gi, c, pl.ds(r, n, stride=dil), :] = o_ref[r, :, c * LANES:(c + 1) * LANES].astype(F32)

    lane = lax.broadcasted_iota(jnp.int32, (OUT_SUB, LANES), 1)
    first_head = lane < A_HEAD_DIM
    for s in range(tm // OUT_SUB):
        rows = slice(s * OUT_SUB, (s + 1) * OUT_SUB)
        stats = (l1_ref[rows, :], l_scr[0, rows, :], l_scr[1, rows, :])
        top = jnp.maximum(jnp.maximum(stats[0], stats[1]), stats[2])
        es = [jnp.exp2(v - top) for v in stats]
        dens = [pltpu.roll(v, LANES - A_HEADS, 1) for v in stats]
        total = es[0] * dens[0] + es[1] * dens[1] + es[2] * dens[2]
        total = jnp.where(lane < A_HEADS, total, 1.0)
        ws = [e / total for e in es]
        ua = []
        for c in range(N_CHUNK):
            cols = slice(c * LANES, (c + 1) * LANES)
            outs = (o1_ref[rows, cols].astype(F32), o_scr[0, c, rows, :], o_scr[1, c, rows, :])
            ya = jnp.zeros((OUT_SUB, LANES), F32)
            for g in range(3):
                w_pair = jnp.where(first_head, ws[g][:, 2 * c:2 * c + 1], ws[g][:, 2 * c + 1:2 * c + 2])
                ya = ya + w_pair * outs[g]
            ua.append(_times_silu_of_half(ya, za_ref[rows, cols].astype(F32)).astype(BF16))
        ua = jnp.concatenate(ua, axis=-1)
        ya_p = jnp.dot(ua, wa_ref[...], preferred_element_type=F32)
        yr_p = jnp.dot(ur_ref[rows, :], wb_ref[...], preferred_element_type=F32)
        ta = jnp.tanh(ga_ref[rows, :].astype(F32))
        tr = jnp.tanh(gr_ref[rows, :].astype(F32))
        merged2 = (ya_p + ya_p * ta) + (yr_p + yr_p * tr)
        out = jnp.dot(merged2.astype(BF16), wo_ref[...], preferred_element_type=F32)
        y_ref[rows, :] = x_ref[rows, :] + (0.5 * mod_ref[2:3, :]) * out


def _output(x2, mod3, layer, b_off, seq, attn, p2, ur, wa, wb, wo):
    t = x2.shape[0]
    per_seq = seq // TM_OUT
    const = dict(pipeline_mode=pl.Buffered(1))
    (o1, l1), (o4, l4), (o16, l16) = attn

    def rows(width, cb=0):
        return pl.BlockSpec((TM_OUT, width), lambda i: (i, cb))

    def dilated(dil, width):
        return pl.BlockSpec((None, dil, TM_OUT // dil, width), lambda i: (i // per_seq, 0, i % per_seq, 0))

    return pl.pallas_call(
        _out_kernel,
        grid=(t // TM_OUT,),
        in_specs=[
            rows(D_MODEL),
            pl.BlockSpec((None, None, 3, D_MODEL), lambda i: (layer, i // per_seq + b_off, 0, 0)),
            rows(A_WIDTH), dilated(4, A_WIDTH), dilated(16, A_WIDTH),
            rows(LANES), dilated(4, LANES), dilated(16, LANES),
            rows(A_WIDTH, CB_ZA),
            rows(R_V_WIDTH),
            rows(D_MODEL, OFF_GA // D_MODEL),
            rows(D_MODEL, OFF_GR // D_MODEL),
            pl.BlockSpec((A_WIDTH, D_MODEL), lambda i: (0, 0), **const),
            pl.BlockSpec((R_V_WIDTH, D_MODEL), lambda i: (0, 0), **const),
            pl.BlockSpec((D_MODEL, D_MODEL), lambda i: (0, 0), **const),
        ],
        out_specs=rows(D_MODEL),
        out_shape=jax.ShapeDtypeStruct((t, D_MODEL), F32),
        scratch_shapes=[pltpu.VMEM((2, N_CHUNK, TM_OUT, LANES), F32), pltpu.VMEM((2, TM_OUT, LANES), F32)],
        compiler_params=_params(("parallel",)),
        name="merge_out_proj",
    )(x2, mod3, o1.reshape(t, A_WIDTH), o4, o16, l1.reshape(t, LANES), l4, l16, p2, ur, p2, p2, wa, wb, wo)


def _mixer_layer(x2, batch, seq, mod3, layer, b_off, lw, tables, e_bd):
    p2, a4, a16 = _inproj(x2, batch, seq, mod3, layer, b_off, lw["norm_g"], lw["w_in"], lw["gq"], lw["gk"], e_bd,
                          tables)
    attn = [_attention(qkv, dil) for qkv, dil in zip((p2.reshape(batch, 1, seq, IN_WIDTH), a4, a16), DILATIONS)]
    nb = batch * seq // RET_BLOCK
    pr = p2.reshape(nb, RET_BLOCK, IN_WIDTH)
    states = _retention_states(pr, lw["dl"]) if seq > RET_BLOCK else None
    ur = _retention(pr, lw["dl"], lw["gn"], states)
    return _output(x2, mod3, layer, b_off, seq, attn, p2, ur, lw["wa"], lw["wb"], lw["wo"])


def _half_gate_columns():
    col = jnp.arange(IN_WIDTH)
    za = (col >= CB_ZA * COL) & (col < (CB_ZA + 1) * COL)
    return jnp.where(za | (col >= OFF_ZR), 0.5, 1.0).astype(F32)[None, :]


def _layer_weights(layer,norm_g, w_in, q_norm_g, k_norm_g, ret_decay_logit, ret_norm_g, w_proj_a, w_proj_b, w_out):
    return {
        "norm_g": norm_g[layer].reshape(1, D_MODEL),
        "w_in": (w_in[layer] * _half_gate_columns()).astype(BF16),
        "gq": (jnp.tile(q_norm_g[layer], A_HEADS) * (A_HEAD_DIM ** -0.5 * LOG2E)).reshape(1, A_WIDTH),
        "gk": jnp.tile(k_norm_g[layer], A_HEADS).reshape(1, A_WIDTH),
        "dl": jnp.broadcast_to(ret_decay_logit[layer].astype(F32).reshape(2 * R_HEADS, 1), (2 * R_HEADS, LANES)),
        "gn": ret_norm_g[layer].reshape(1, R_V_WIDTH),
        "wa": w_proj_a[layer].astype(BF16),
        "wb": w_proj_b[layer].astype(BF16),
        "wo": w_out[layer].astype(BF16),
    }


def kernel(x_prompt, x_sample, c_prompt, c_sample, norm_g, w_ada, b_ada, w_in, q_norm_g, k_norm_g, ret_decay_logit,
           ret_norm_g, w_proj_a, w_proj_b, w_out):
    bp, sp, _ = x_prompt.shape
    bs, ss, _ = x_sample.shape
    pad = (-(bp + bs)) % 8
    c_all = jnp.concatenate([c_prompt, c_sample, jnp.zeros((pad, D_MODEL), F32)], axis=0)
    mod3 = _modulation(c_all, w_ada, b_ada).reshape(DEPTH, bp + bs + pad, 3, D_MODEL)
    head_of_lane = jnp.arange(E_BLK) // A_HEAD_DIM
    e_bd = (head_of_lane[:, None] == head_of_lane[None, :]).astype(BF16)
    tab_p = _rope_tables(sp)
    tab_s = tab_p if ss == sp else _rope_tables(ss)
    yp = x_prompt.reshape(bp * sp, D_MODEL)
    ys = x_sample.reshape(bs * ss, D_MODEL)
    for layer in range(DEPTH):
        lw = _layer_weights(layer, norm_g, w_in, q_norm_g, k_norm_g, ret_decay_logit, ret_norm_g, w_proj_a, w_proj_b,
                            w_out)
        yp = _mixer_layer(yp, bp, sp, mod3, layer, 0, lw, tab_p, e_bd)
        ys = _mixer_layer(ys, bs, ss, mod3, layer, bp, lw, tab_s, e_bd)
    return yp.reshape(bp, sp, D_MODEL), ys.reshape(bs, ss, D_MODEL)
```

```python
import functools
import math

import jax
import jax.numpy as jnp
from jax import lax
from jax.experimental import pallas as pl
from jax.experimental.pallas import tpu as pltpu

D_MODEL = 1024
DEPTH = 4
A_HEADS = 8
A_HEAD_DIM = 64
A_WIDTH = A_HEADS * A_HEAD_DIM
A_ROT_HALF = A_HEAD_DIM // 8
ATT_THETA = 500000.0
DILATIONS = (1, 4, 16)
HALF_WIN = 64
R_HEADS = 4
R_KEY_DIM = 128
R_VAL_DIM = 256
R_V_WIDTH = R_HEADS * R_VAL_DIM
RET_THETA = 10000.0
RET_CHUNK = 128
IN_WIDTH = 4 * A_WIDTH + 2 * R_HEADS * R_KEY_DIM + 2 * R_V_WIDTH + 2 * D_MODEL
EPS = 1e-6
NEG = -1e30
LOG2E = 1.4426950408889634

LANES = 128
COL = 512
N_COL = IN_WIDTH // COL
N_CHUNK = COL // LANES
E_BLK = 256
CB_QA, CB_KA, CB_VA, CB_ZA, CB_QR, CB_KR = 0, 1, 2, 3, 4, 5
OFF_VR, OFF_ZR, OFF_GA, OFF_GR = 3072, 4096, 5120, 6144
QKV_W = 3 * A_WIDTH
ROPE_TABLE_W = 5 * LANES

TM_IN = 512
TM_OUT = 1024
OUT_SUB = 256
TQ_ATT = 2048
Q_SUB = 128
K_WIN = Q_SUB + 2 * HALF_WIN
ATT_UNROLL = 16
RET_BLOCK = 2048
RET_HEADS_PER_STEP = 2
VMEM_LIMIT = 56 * 1024 * 1024
VMEM_LIMIT_OUT = 60 * 1024 * 1024

F32 = jnp.float32
BF16 = jnp.bfloat16


def _silu(v):
    h = 0.5 * v
    return h + h * jnp.tanh(h)


def _times_silu_of_half(u, h):
    uh = u * h
    return uh + uh * jnp.tanh(h)


def _params(sem, vmem=VMEM_LIMIT):
    return pltpu.CompilerParams(dimension_semantics=sem, vmem_limit_bytes=vmem)


def _mod_kernel(c_ref, w_ref, b_ref, o_ref):
    s = _silu(c_ref[...])
    o_ref[...] = jnp.dot(s, w_ref[...], preferred_element_type=F32, precision=lax.Precision.HIGHEST) + b_ref[...]


def _modulation(c_all, w_ada, b_ada):
    nb = c_all.shape[0]
    return pl.pallas_call(
        _mod_kernel,
        grid=(DEPTH, 3),
        in_specs=[
            pl.BlockSpec((nb, D_MODEL), lambda l, j: (0, 0)),
            pl.BlockSpec((None, D_MODEL, D_MODEL), lambda l, j: (l, 0, j)),
            pl.BlockSpec((None, 1, D_MODEL), lambda l, j: (l, 0, j)),
        ],
        out_specs=pl.BlockSpec((None, nb, D_MODEL), lambda l, j: (l, 0, j)),
        out_shape=jax.ShapeDtypeStruct((DEPTH, nb, 3 * D_MODEL), F32),
        compiler_params=_params(("parallel", "parallel")),
        name="adaln_mod",
    )(c_all, w_ada, b_ada.reshape(DEPTH, 1, 3 * D_MODEL))


def _rope_tables(seq):
    pos = jnp.arange(seq, dtype=F32)
    fa = jnp.exp(-math.log(ATT_THETA) * jnp.arange(A_ROT_HALF, dtype=F32) / A_ROT_HALF)
    ang = pos[:, None] * fa[None, :]
    cos, sin = jnp.cos(ang), jnp.sin(ang)
    rest = A_HEAD_DIM - 2 * A_ROT_HALF
    one = jnp.ones((seq, rest), F32)
    zero = jnp.zeros((seq, rest), F32)
    z8 = jnp.zeros((seq, A_ROT_HALF), F32)
    att_c = jnp.tile(jnp.concatenate([cos, cos, one], 1), (1, 2))
    att_a = jnp.tile(jnp.concatenate([-sin, z8, zero], 1), (1, 2))
    att_b = jnp.tile(jnp.concatenate([z8, sin, zero], 1), (1, 2))
    half = R_KEY_DIM // 2
    fr = jnp.exp(-math.log(RET_THETA) * jnp.arange(half, dtype=F32) / half)
    ang = pos[:, None] * fr[None, :]
    cos, sin = jnp.cos(ang), jnp.sin(ang)
    ret_c = jnp.concatenate([cos, cos], 1)
    ret_s = jnp.concatenate([-sin, sin], 1)
    return jnp.concatenate([att_c, att_a, att_b, ret_c, ret_s], 1)


def _inproj_kernel(x_ref, mod_ref, g_ref, w_ref, gq_ref, gk_ref, e_ref, tab_ref, o_ref, a4_ref, a16_ref, rows_scr):
    x = x_ref[...]
    ms = jnp.mean(x * x, axis=-1, keepdims=True)
    y = x * lax.rsqrt(ms + EPS) * g_ref[...]
    h = (y * (1.0 + mod_ref[1:2, :]) + mod_ref[0:1, :]).astype(BF16)

    att_c = tab_ref[:, 0 * LANES:1 * LANES]
    att_a = tab_ref[:, 1 * LANES:2 * LANES]
    att_b = tab_ref[:, 2 * LANES:3 * LANES]
    ret_c = tab_ref[:, 3 * LANES:4 * LANES]
    ret_s = tab_ref[:, 4 * LANES:5 * LANES]

    def emit_qkv(j, c, val):
        lo = j * COL + c * LANES
        o_ref[:, lo:lo + LANES] = val.astype(BF16)
        rows_scr[0, c] = val
        n4 = TM_IN // 4
        n16 = TM_IN // 16
        for r4 in range(4):
            by4 = rows_scr[0, c, pl.ds(r4, n4, stride=4), :]
            a4_ref[r4, :, lo:lo + LANES] = by4.astype(BF16)
            rows_scr[1, c, r4 * n4:(r4 + 1) * n4, :] = by4
        for r4 in range(4):
            for r in range(4):
                by16 = rows_scr[1, c, pl.ds(r4 * n4 + r, n16, stride=4), :]
                a16_ref[4 * r + r4, :, lo:lo + LANES] = by16.astype(BF16)

    for j in range(N_COL):
        acc = jnp.dot(h, w_ref[:, j * COL:(j + 1) * COL], preferred_element_type=F32)
        if j in (CB_QA, CB_KA):
            sq = (acc * acc).astype(BF16)
            ss = jnp.concatenate(
                [jnp.dot(sq[:, s * E_BLK:(s + 1) * E_BLK], e_ref[...], preferred_element_type=F32)
                 for s in range(COL // E_BLK)], axis=-1)
            gain = gq_ref[...] if j == CB_QA else gk_ref[...]
            yn = acc * lax.rsqrt(ss * (1.0 / A_HEAD_DIM) + EPS) * gain
            for c in range(N_CHUNK):
                xc = yn[:, c * LANES:(c + 1) * LANES]
                rot = (xc * att_c + pltpu.roll(xc, LANES - A_ROT_HALF, 1) * att_a
                       + pltpu.roll(xc, A_ROT_HALF, 1) * att_b)
                emit_qkv(j, c, rot)
        elif j == CB_VA:
            for c in range(N_CHUNK):
                emit_qkv(j, c, acc[:, c * LANES:(c + 1) * LANES])
        elif j in (CB_QR, CB_KR):
            for c in range(N_CHUNK):
                xc = acc[:, c * LANES:(c + 1) * LANES]
                rot = xc * ret_c + pltpu.roll(xc, R_KEY_DIM // 2, 1) * ret_s
                if j == CB_KR:
                    rot = rot * (R_KEY_DIM ** -0.5)
                o_ref[:, j * COL + c * LANES:j * COL + (c + 1) * LANES] = rot.astype(BF16)
        else:
            o_ref[:, j * COL:(j + 1) * COL] = acc.astype(BF16)


def _inproj(x2, batch, seq, mod3, layer, b_off, norm_g, w_in, gq, gk, e_bd, tables):
    t = x2.shape[0]
    per_seq = seq // TM_IN
    const = dict(pipeline_mode=pl.Buffered(1))

    def dilated(dil):
        return pl.BlockSpec((None, dil, TM_IN // dil, QKV_W), lambda i: (i // per_seq, 0, i % per_seq, 0))

    return pl.pallas_call(
        _inproj_kernel,
        grid=(t // TM_IN,),
        in_specs=[
            pl.BlockSpec((TM_IN, D_MODEL), lambda i: (i, 0)),
            pl.BlockSpec((None, None, 3, D_MODEL), lambda i: (layer, i // per_seq + b_off, 0, 0)),
            pl.BlockSpec((1, D_MODEL), lambda i: (0, 0)),
            pl.BlockSpec((D_MODEL, IN_WIDTH), lambda i: (0, 0), **const),
            pl.BlockSpec((1, COL), lambda i: (0, 0)),
            pl.BlockSpec((1, COL), lambda i: (0, 0)),
            pl.BlockSpec((E_BLK, E_BLK), lambda i: (0, 0), **const),
            pl.BlockSpec((TM_IN, ROPE_TABLE_W), lambda i: (i % per_seq, 0)),
        ],
        out_specs=[pl.BlockSpec((TM_IN, IN_WIDTH), lambda i: (i, 0)), dilated(4), dilated(16)],
        out_shape=[
            jax.ShapeDtypeStruct((t, IN_WIDTH), BF16),
            jax.ShapeDtypeStruct((batch, 4, seq // 4, QKV_W), BF16),
            jax.ShapeDtypeStruct((batch, 16, seq // 16, QKV_W), BF16),
        ],
        scratch_shapes=[pltpu.VMEM((2, N_CHUNK, TM_IN, LANES), F32)],
        compiler_params=_params(("parallel",)),
        name="in_proj",
    )(x2, mod3, norm_g, w_in, gq, gk, e_bd, tables)


def _attn_kernel(q_ref, kp_ref, kc_ref, kn_ref, vp_ref, vc_ref, vn_ref, o_ref, st_ref, kbuf, vbuf, *, tq, nr, length):
    kbuf[:, 0:HALF_WIN, :] = kp_ref[...]
    kbuf[:, HALF_WIN:HALF_WIN + tq, :] = kc_ref[...]
    kbuf[:, HALF_WIN + tq:, :] = kn_ref[...]
    key_lane = lax.broadcasted_iota(jnp.int32, (nr, tq + 2 * HALF_WIN, LANES), 2)
    for p in range(A_WIDTH // LANES):
        src = slice(p * LANES, (p + 1) * LANES)
        dst = slice(2 * p * LANES, (2 * p + 1) * LANES)
        vbuf[:, 0:HALF_WIN, dst] = vp_ref[:, :, src]
        vbuf[:, HALF_WIN:HALF_WIN + tq, dst] = vc_ref[:, :, src]
        vbuf[:, HALF_WIN + tq:, dst] = vn_ref[:, :, src]
        den_lanes = (key_lane == A_HEADS + 2 * p) | (key_lane == A_HEADS + 2 * p + 1)
        vbuf[:, :, (2 * p + 1) * LANES:(2 * p + 2) * LANES] = jnp.where(den_lanes, 1.0, 0.0).astype(BF16)
    q0 = pl.program_id(2) * tq

    kj = lax.broadcasted_iota(jnp.int32, (Q_SUB, K_WIN), 1)
    qi = lax.broadcasted_iota(jnp.int32, (Q_SUB, K_WIN), 0)
    band = (kj >= qi) & (kj <= qi + 2 * HALF_WIN)
    lane = lax.broadcasted_iota(jnp.int32, (Q_SUB, LANES), 1)
    first_head = lane < A_HEAD_DIM
    odd_lane = (lane & 1) == 1
    sub_per_res = tq // Q_SUB

    def sub_block(t, carry):
        ri = t // sub_per_res
        r0 = pl.multiple_of((t % sub_per_res) * Q_SUB, Q_SUB)
        kpos = kj + (q0 + r0 - HALF_WIN)
        bias = jnp.where(band & (kpos >= 0) & (kpos < length), 0.0, NEG)
        bias = jnp.concatenate([bias, bias], axis=0)
        stat = jnp.zeros((Q_SUB, LANES), F32)
        for p in range(A_WIDTH // LANES):
            cols = slice(p * LANES, (p + 1) * LANES)
            q2 = q_ref[ri, pl.ds(r0, Q_SUB), cols]
            k2 = kbuf[ri, pl.ds(r0, K_WIN), cols]
            v2 = vbuf[ri, pl.ds(r0, K_WIN), 2 * p * LANES:(2 * p + 2) * LANES]
            zero = jnp.zeros_like(q2)
            qs = jnp.concatenate([jnp.where(first_head, q2, zero), jnp.where(first_head, zero, q2)], axis=0)
            sc = lax.dot_general(qs, k2, (((1,), (1,)), ((), ())), preferred_element_type=F32) + bias
            m = jnp.max(sc, axis=-1, keepdims=True)
            pr = jnp.exp2(sc - m)
            pvd = jnp.dot(pr.astype(BF16), v2, preferred_element_type=F32)
            pv, den = pvd[:, :LANES], pvd[:, LANES:]
            stat = stat + jnp.where(odd_lane, den[Q_SUB:], den[:Q_SUB])
            stat = jnp.where(lane == 2 * p, m[:Q_SUB], stat)
            stat = jnp.where(lane == 2 * p + 1, m[Q_SUB:], stat)
            o_ref[ri, pl.ds(r0, Q_SUB), cols] = jnp.where(first_head, pv[:Q_SUB], pv[Q_SUB:]).astype(o_ref.dtype)
        st_ref[ri, pl.ds(r0, Q_SUB), :] = stat
        return carry

    lax.fori_loop(0, nr * sub_per_res, sub_block, 0, unroll=ATT_UNROLL)


def _attention(qkv, dil):
    b, _, length, _ = qkv.shape
    tq = min(TQ_ATT, length)
    nr = min(dil, TQ_ATT // tq)
    nq = length // tq
    halo_per_q = tq // HALF_WIN
    last_halo = length // HALF_WIN - 1

    def cur(cb):
        return pl.BlockSpec((None, nr, tq, COL), lambda bi, r, i: (bi, r, i, cb))

    def prev(cb):
        return pl.BlockSpec((None, nr, HALF_WIN, COL),
                            lambda bi, r, i: (bi, r, jnp.maximum(i * halo_per_q - 1, 0), cb))

    def nxt(cb):
        return pl.BlockSpec((None, nr, HALF_WIN, COL),
                            lambda bi, r, i: (bi, r, jnp.minimum((i + 1) * halo_per_q, last_halo), cb))

    return pl.pallas_call(
        functools.partial(_attn_kernel, tq=tq, nr=nr, length=length),
        grid=(b, dil // nr, nq),
        in_specs=[cur(CB_QA), prev(CB_KA), cur(CB_KA), nxt(CB_KA), prev(CB_VA), cur(CB_VA), nxt(CB_VA)],
        out_specs=[
            pl.BlockSpec((None, nr, tq, A_WIDTH), lambda bi, r, i: (bi, r, i, 0)),
            pl.BlockSpec((None, nr, tq, LANES), lambda bi, r, i: (bi, r, i, 0)),
        ],
        out_shape=[
            jax.ShapeDtypeStruct((b, dil, length, A_WIDTH), BF16),
            jax.ShapeDtypeStruct((b, dil, length, LANES), F32),
        ],
        scratch_shapes=[pltpu.VMEM((nr, tq + 2 * HALF_WIN, COL), BF16),
                        pltpu.VMEM((nr, tq + 2 * HALF_WIN, 2 * COL), BF16)],
        compiler_params=_params(("parallel", "parallel", "parallel")),
        name=f"window_attn_d{dil}",
    )(qkv, qkv, qkv, qkv, qkv, qkv, qkv)


def _log_sigmoid(v):
    return jnp.minimum(v, 0.0) - jnp.log1p(jnp.exp(-jnp.abs(v)))


def _decays(dl_ref, head):
    lg_f = _log_sigmoid(dl_ref[pl.ds(head, 1), :])
    lg_b = _log_sigmoid(dl_ref[pl.ds(head + R_HEADS, 1), :])
    return lg_f, lg_b


def _chunk_tables(lg_f, lg_b):
    c = RET_CHUNK
    row = lax.broadcasted_iota(jnp.int32, (c, c), 0).astype(F32)
    col = lax.broadcasted_iota(jnp.int32, (c, c), 1).astype(F32)
    rel = row - col
    intra = (jnp.where(rel >= 0, jnp.exp(jnp.maximum(rel, 0.0) * lg_f), 0.0)
             + jnp.where(rel <= 0, jnp.exp(jnp.maximum(-rel, 0.0) * lg_b), 0.0))
    kd_f = jnp.exp((c - 1 - row) * lg_f)
    kd_b = jnp.exp(row * lg_b)
    qd_f = jnp.exp((row + 1.0) * lg_f)
    qd_b = jnp.exp((c - row) * lg_b)
    cd_f = jnp.exp(c * lg_f)
    cd_b = jnp.exp(c * lg_b)
    return intra, kd_f, kd_b, qd_f, qd_b, cd_f, cd_b


def _wide(v):
    return jnp.concatenate([v, v], axis=-1)


def _delta(k_chunk, kd, v_chunk):
    kt = (k_chunk.astype(F32) * kd).T.astype(BF16)
    return jnp.dot(kt, v_chunk, preferred_element_type=F32)


def _ret_state_kernel(dl_ref, kf_ref, vf_ref, kb_ref, vb_ref, fin_ref, bin_ref, sf, sb):
    head = pl.program_id(0)

    @pl.when(pl.program_id(1) == 0)
    def _():
        sf[...] = jnp.zeros_like(sf)
        sb[...] = jnp.zeros_like(sb)

    fin_ref[...] = sf[...]
    bin_ref[...] = sb[...]
    lg_f, lg_b = _decays(dl_ref, head)
    _, kd_f, kd_b, _, _, cd_f, cd_b = _chunk_tables(lg_f, lg_b)
    cd_f, cd_b = _wide(cd_f), _wide(cd_b)
    n_chunks = RET_BLOCK // RET_CHUNK

    def scan(t, carry):
        rf = pl.ds(pl.multiple_of(t * RET_CHUNK, RET_CHUNK), RET_CHUNK)
        rb = pl.ds(pl.multiple_of((n_chunks - 1 - t) * RET_CHUNK, RET_CHUNK), RET_CHUNK)
        sf[...] = sf[...] * cd_f + _delta(kf_ref[rf, :], kd_f, vf_ref[rf, :])
        sb[...] = sb[...] * cd_b + _delta(kb_ref[rb, :], kd_b, vb_ref[rb, :])
        return carry

    lax.fori_loop(0, n_chunks, scan, 0, unroll=True)


def _retention_states(p3, dl_tile):
    nb = p3.shape[0]
    k_cb = (CB_KR * COL) // R_KEY_DIM
    v_cb = OFF_VR // R_VAL_DIM
    shape = jax.ShapeDtypeStruct((nb, R_HEADS, R_KEY_DIM, R_VAL_DIM), F32)
    return pl.pallas_call(
        _ret_state_kernel,
        grid=(R_HEADS, nb),
        in_specs=[
            pl.BlockSpec((2 * R_HEADS, LANES), lambda h, j: (0, 0)),
            pl.BlockSpec((None, RET_BLOCK, R_KEY_DIM), lambda h, j: (j, 0, k_cb + h)),
            pl.BlockSpec((None, RET_BLOCK, R_VAL_DIM), lambda h, j: (j, 0, v_cb + h)),
            pl.BlockSpec((None, RET_BLOCK, R_KEY_DIM), lambda h, j: (nb - 1 - j, 0, k_cb + h)),
            pl.BlockSpec((None, RET_BLOCK, R_VAL_DIM), lambda h, j: (nb - 1 - j, 0, v_cb + h)),
        ],
        out_specs=[
            pl.BlockSpec((None, None, R_KEY_DIM, R_VAL_DIM), lambda h, j: (j, h, 0, 0)),
            pl.BlockSpec((None, None, R_KEY_DIM, R_VAL_DIM), lambda h, j: (nb - 1 - j, h, 0, 0)),
        ],
        out_shape=[shape, shape],
        scratch_shapes=[pltpu.VMEM((R_KEY_DIM, R_VAL_DIM), F32), pltpu.VMEM((R_KEY_DIM, R_VAL_DIM), F32)],
        compiler_params=_params(("parallel", "arbitrary")),
        name="retention_states",
    )(dl_tile, p3, p3, p3, p3)


def _ret_kernel(*refs, has_state):
    if has_state:
        dl_ref, gn_ref, q_ref, k_ref, v_ref, z_ref, fin_ref, bin_ref, o_ref, sf, sb, s_all = refs
    else:
        dl_ref, gn_ref, q_ref, k_ref, v_ref, z_ref, o_ref, sf, sb, s_all = refs
    n_chunks = RET_BLOCK // RET_CHUNK

    for hh in range(RET_HEADS_PER_STEP):
        head = pl.program_id(1) * RET_HEADS_PER_STEP + hh
        kcols = slice(hh * R_KEY_DIM, (hh + 1) * R_KEY_DIM)
        vcols = slice(hh * R_VAL_DIM, (hh + 1) * R_VAL_DIM)
        lg_f, lg_b = _decays(dl_ref, head)
        intra, kd_f, kd_b, qd_f, qd_b, cd_f, cd_b = _chunk_tables(lg_f, lg_b)
        cd_f, cd_b = _wide(cd_f), _wide(cd_b)
        gn = gn_ref[:, vcols]
        if has_state:
            sf[hh] = fin_ref[hh]
            sb[hh] = bin_ref[hh]
        else:
            sf[hh] = jnp.zeros((R_KEY_DIM, R_VAL_DIM), F32)
            sb[hh] = jnp.zeros((R_KEY_DIM, R_VAL_DIM), F32)

        for t in range(n_chunks):
            tb = n_chunks - 1 - t
            rf = slice(t * RET_CHUNK, (t + 1) * RET_CHUNK)
            rb = slice(tb * RET_CHUNK, (tb + 1) * RET_CHUNK)
            state_f = sf[hh]
            s_all[hh, t, 0:R_KEY_DIM, :] = state_f.astype(BF16)
            sf[hh] = state_f * cd_f + _delta(k_ref[rf, kcols], kd_f, v_ref[rf, vcols])
            state_b = sb[hh]
            s_all[hh, tb, R_KEY_DIM:2 * R_KEY_DIM, :] = state_b.astype(BF16)
            sb[hh] = state_b * cd_b + _delta(k_ref[rb, kcols], kd_b, v_ref[rb, vcols])

        for n in range(n_chunks):
            rows = slice(n * RET_CHUNK, (n + 1) * RET_CHUNK)
            q, k, v = q_ref[rows, kcols], k_ref[rows, kcols], v_ref[rows, vcols]
            qf = q.astype(F32)
            qq = jnp.concatenate([(qf * qd_f).astype(BF16), (qf * qd_b).astype(BF16)], axis=-1)
            sc = lax.dot_general(q, k, (((1,), (1,)), ((), ())), preferred_element_type=F32) * intra
            y = (jnp.dot(sc.astype(BF16), v, preferred_element_type=F32)
                 + jnp.dot(qq, s_all[hh, n], preferred_element_type=F32))
            mu = jnp.mean(y, axis=-1, keepdims=True)
            yc = y - mu
            var = jnp.mean(yc * yc, axis=-1, keepdims=True)
            yn = yc * lax.rsqrt(var + EPS) * gn
            o_ref[rows, vcols] = _times_silu_of_half(yn, z_ref[rows, vcols].astype(F32)).astype(BF16)


def _retention(p3, dl_tile, gn, states):
    nb = p3.shape[0]
    hps = RET_HEADS_PER_STEP
    kw, vw = hps * R_KEY_DIM, hps * R_VAL_DIM
    q_cb = (CB_QR * COL) // kw
    k_cb = (CB_KR * COL) // kw
    v_cb = OFF_VR // vw
    z_cb = OFF_ZR // vw
    in_specs = [
        pl.BlockSpec((2 * R_HEADS, LANES), lambda b, h: (0, 0)),
        pl.BlockSpec((1, vw), lambda b, h: (0, h)),
        pl.BlockSpec((None, RET_BLOCK, kw), lambda b, h: (b, 0, q_cb + h)),
        pl.BlockSpec((None, RET_BLOCK, kw), lambda b, h: (b, 0, k_cb + h)),
        pl.BlockSpec((None, RET_BLOCK, vw), lambda b, h: (b, 0, v_cb + h)),
        pl.BlockSpec((None, RET_BLOCK, vw), lambda b, h: (b, 0, z_cb + h)),
    ]
    args = [dl_tile, gn, p3, p3, p3, p3]
    if states is not None:
        st = pl.BlockSpec((None, hps, R_KEY_DIM, R_VAL_DIM), lambda b, h: (b, h, 0, 0))
        in_specs += [st, st]
        args += list(states)
    n_chunks = RET_BLOCK // RET_CHUNK
    out = pl.pallas_call(
        functools.partial(_ret_kernel, has_state=states is not None),
        grid=(nb, R_HEADS // hps),
        in_specs=in_specs,
        out_specs=pl.BlockSpec((None, RET_BLOCK, vw), lambda b, h: (b, 0, h)),
        out_shape=jax.ShapeDtypeStruct((nb, RET_BLOCK, R_V_WIDTH), BF16),
        scratch_shapes=[
            pltpu.VMEM((hps, R_KEY_DIM, R_VAL_DIM), F32),
            pltpu.VMEM((hps, R_KEY_DIM, R_VAL_DIM), F32),
            pltpu.VMEM((hps, n_chunks, 2 * R_KEY_DIM, R_VAL_DIM), BF16),
        ],
        compiler_params=_params(("parallel", "parallel")),
        name="retention",
    )(*args)
    return out.reshape(nb * RET_BLOCK, R_V_WIDTH)


def _out_kernel(x_ref, mod_ref, o1_ref, o4_ref, o16_ref, l1_ref, l4_ref, l16_ref, za_ref, ur_ref, ga_ref, gr_ref,
                wa_ref, wb_ref, wo_ref, y_ref, o_scr, l_scr, yr_scr):
    tm = x_ref.shape[0]
    yr_scr[...] = jnp.dot(ur_ref[...], wb_ref[...], preferred_element_type=F32)
    for gi, (dil, o_ref, l_ref) in enumerate(((4, o4_ref, l4_ref), (16, o16_ref, l16_ref))):
        n = tm // dil
        for r in range(dil):
            l_scr[gi, pl.ds(r, n, stride=dil), :] = l_ref[r]
            for c in range(N_CHUNK):
                o_scr[gi, c, pl.ds(r, n, stride=dil), :] = o_ref[r, :, c * LANES:(c + 1) * LANES].astype(F32)

    lane = lax.broadcasted_iota(jnp.int32, (OUT_SUB, LANES), 1)
    first_head = lane < A_HEAD_DIM
    for s in range(tm // OUT_SUB):
        rows = slice(s * OUT_SUB, (s + 1) * OUT_SUB)
        stats = (l1_ref[rows, :], l_scr[0, rows, :], l_scr[1, rows, :])
        top = jnp.maximum(jnp.maximum(stats[0], stats[1]), stats[2])
        es = [jnp.exp2(v - top) for v in stats]
        dens = [pltpu.roll(v, LANES - A_HEADS, 1) for v in stats]
        total = es[0] * dens[0] + es[1] * dens[1] + es[2] * dens[2]
        total = jnp.where(lane < A_HEADS, total, 1.0)
        ws = [e / total for e in es]
        ua = []
        for c in range(N_CHUNK):
            cols = slice(c * LANES, (c + 1) * LANES)
            outs = (o1_ref[rows, cols].astype(F32), o_scr[0, c, rows, :], o_scr[1, c, rows, :])
            ya = jnp.zeros((OUT_SUB, LANES), F32)
            for g in range(3):
                w_pair = jnp.where(first_head, ws[g][:, 2 * c:2 * c + 1], ws[g][:, 2 * c + 1:2 * c + 2])
                ya = ya + w_pair * outs[g]
            ua.append(_times_silu_of_half(ya, za_ref[rows, cols].astype(F32)).astype(BF16))
        ua = jnp.concatenate(ua, axis=-1)
        ya_p = jnp.dot(ua, wa_ref[...], preferred_element_type=F32)
        yr_p = yr_scr[rows, :]
        ta = jnp.tanh(ga_ref[rows, :].astype(F32))
        tr = jnp.tanh(gr_ref[rows, :].astype(F32))
        merged2 = (ya_p + ya_p * ta) + (yr_p + yr_p * tr)
        out = jnp.dot(merged2.astype(BF16), wo_ref[...], preferred_element_type=F32)
        y_ref[rows, :] = x_ref[rows, :] + (0.5 * mod_ref[2:3, :]) * out


def _output(x2, mod3, layer, b_off, seq, attn, p2, ur, wa, wb, wo):
    t = x2.shape[0]
    per_seq = seq // TM_OUT
    const = dict(pipeline_mode=pl.Buffered(1))
    (o1, l1), (o4, l4), (o16, l16) = attn

    def rows(width, cb=0):
        return pl.BlockSpec((TM_OUT, width), lambda i: (i, cb))

    def dilated(dil, width):
        return pl.BlockSpec((None, dil, TM_OUT // dil, width), lambda i: (i // per_seq, 0, i % per_seq, 0))

    return pl.pallas_call(
        _out_kernel,
        grid=(t // TM_OUT,),
        in_specs=[
            rows(D_MODEL),
            pl.BlockSpec((None, None, 3, D_MODEL), lambda i: (layer, i // per_seq + b_off, 0, 0)),
            rows(A_WIDTH), dilated(4, A_WIDTH), dilated(16, A_WIDTH),
            rows(LANES), dilated(4, LANES), dilated(16, LANES),
            rows(A_WIDTH, CB_ZA),
            rows(R_V_WIDTH),
            rows(D_MODEL, OFF_GA // D_MODEL),
            rows(D_MODEL, OFF_GR // D_MODEL),
            pl.BlockSpec((A_WIDTH, D_MODEL), lambda i: (0, 0), **const),
            pl.BlockSpec((R_V_WIDTH, D_MODEL), lambda i: (0, 0), **const),
            pl.BlockSpec((D_MODEL, D_MODEL), lambda i: (0, 0), **const),
        ],
        out_specs=rows(D_MODEL),
        out_shape=jax.ShapeDtypeStruct((t, D_MODEL), F32),
        scratch_shapes=[
            pltpu.VMEM((2, N_CHUNK, TM_OUT, LANES), F32),
            pltpu.VMEM((2, TM_OUT, LANES), F32),
            pltpu.VMEM((TM_OUT, D_MODEL), F32),
        ],
        compiler_params=_params(("parallel",), vmem=VMEM_LIMIT_OUT),
        name="merge_out_proj",
    )(x2, mod3, o1.reshape(t, A_WIDTH), o4, o16, l1.reshape(t, LANES), l4, l16, p2, ur, p2, p2, wa, wb, wo)


def _mixer_layer(x2, batch, seq, mod3, layer, b_off, lw, tables, e_bd):
    p2, a4, a16 = _inproj(x2, batch, seq, mod3, layer, b_off, lw["norm_g"], lw["w_in"], lw["gq"], lw["gk"], e_bd,
                          tables)
    attn = [_attention(qkv, dil) for qkv, dil in zip((p2.reshape(batch, 1, seq, IN_WIDTH), a4, a16), DILATIONS)]
    nb = batch * seq // RET_BLOCK
    pr = p2.reshape(nb, RET_BLOCK, IN_WIDTH)
    states = _retention_states(pr, lw["dl"]) if seq > RET_BLOCK else None
    ur = _retention(pr, lw["dl"], lw["gn"], states)
    return _output(x2, mod3, layer, b_off, seq, attn, p2, ur, lw["wa"], lw["wb"], lw["wo"])


def _half_gate_columns():
    col = jnp.arange(IN_WIDTH)
    za = (col >= CB_ZA * COL) & (col < (CB_ZA + 1) * COL)
    return jnp.where(za | (col >= OFF_ZR), 0.5, 1.0).astype(F32)[None, :]


def _layer_weights(layer,norm_g, w_in, q_norm_g, k_norm_g, ret_decay_logit, ret_norm_g, w_proj_a, w_proj_b, w_out):
    return {
        "norm_g": norm_g[layer].reshape(1, D_MODEL),
        "w_in": (w_in[layer] * _half_gate_columns()).astype(BF16),
        "gq": (jnp.tile(q_norm_g[layer], A_HEADS) * (A_HEAD_DIM ** -0.5 * LOG2E)).reshape(1, A_WIDTH),
        "gk": jnp.tile(k_norm_g[layer], A_HEADS).reshape(1, A_WIDTH),
        "dl": jnp.broadcast_to(ret_decay_logit[layer].astype(F32).reshape(2 * R_HEADS, 1), (2 * R_HEADS, LANES)),
        "gn": ret_norm_g[layer].reshape(1, R_V_WIDTH),
        "wa": w_proj_a[layer].astype(BF16),
        "wb": w_proj_b[layer].astype(BF16),
        "wo": w_out[layer].astype(BF16),
    }


def kernel(x_prompt, x_sample, c_prompt, c_sample, norm_g, w_ada, b_ada, w_in, q_norm_g, k_norm_g, ret_decay_logit,
           ret_norm_g, w_proj_a, w_proj_b, w_out):
    bp, sp, _ = x_prompt.shape
    bs, ss, _ = x_sample.shape
    pad = (-(bp + bs)) % 8
    c_all = jnp.concatenate([c_prompt, c_sample, jnp.zeros((pad, D_MODEL), F32)], axis=0)
    mod3 = _modulation(c_all, w_ada, b_ada).reshape(DEPTH, bp + bs + pad, 3, D_MODEL)
    head_of_lane = jnp.arange(E_BLK) // A_HEAD_DIM
    e_bd = (head_of_lane[:, None] == head_of_lane[None, :]).astype(BF16)
    tab_p = _rope_tables(sp)
    tab_s = tab_p if ss == sp else _rope_tables(ss)
    yp = x_prompt.reshape(bp * sp, D_MODEL)
    ys = x_sample.reshape(bs * ss, D_MODEL)
    for layer in range(DEPTH):
        lw = _layer_weights(layer, norm_g, w_in, q_norm_g, k_norm_g, ret_decay_logit, ret_norm_g, w_proj_a, w_proj_b,
                            w_out)
        yp = _mixer_layer(yp, bp, sp, mod3, layer, 0, lw, tab_p, e_bd)
        ys = _mixer_layer(ys, bs, ss, mod3, layer, bp, lw, tab_s, e_bd)
    return yp.reshape(bp, sp, D_MODEL), ys.reshape(bs, ss, D_MODEL)
```

```python
import functools
import math

import jax
import jax.numpy as jnp
from jax import lax
from jax.experimental import pallas as pl
from jax.experimental.pallas import tpu as pltpu

D_MODEL = 1024
DEPTH = 4
A_HEADS = 8
A_HEAD_DIM = 64
A_WIDTH = A_HEADS * A_HEAD_DIM
A_ROT_HALF = A_HEAD_DIM // 8
ATT_THETA = 500000.0
DILATIONS = (1, 4, 16)
HALF_WIN = 64
R_HEADS = 4
R_KEY_DIM = 128
R_VAL_DIM = 256
R_V_WIDTH = R_HEADS * R_VAL_DIM
RET_THETA = 10000.0
RET_CHUNK = 128
IN_WIDTH = 4 * A_WIDTH + 2 * R_HEADS * R_KEY_DIM + 2 * R_V_WIDTH + 2 * D_MODEL
EPS = 1e-6
NEG = -1e30
LOG2E = 1.4426950408889634

LANES = 128
COL = 512
N_COL = IN_WIDTH // COL
N_CHUNK = COL // LANES
E_BLK = 256
CB_QA, CB_KA, CB_VA, CB_ZA, CB_QR, CB_KR = 0, 1, 2, 3, 4, 5
OFF_VR, OFF_ZR, OFF_GA, OFF_GR = 3072, 4096, 5120, 6144
QKV_W = 3 * A_WIDTH
ROPE_TABLE_W = 5 * LANES

TM_IN = 512
TM_OUT = 1024
OUT_SUB = 256
TQ_ATT = 2048
Q_SUB = 128
K_WIN = Q_SUB + 2 * HALF_WIN
ATT_UNROLL = 16
RET_BLOCK = 2048
RET_HEADS_PER_STEP = 2
VMEM_LIMIT = 56 * 1024 * 1024
VMEM_LIMIT_OUT = 60 * 1024 * 1024

F32 = jnp.float32
BF16 = jnp.bfloat16


def _silu(v):
    h = 0.5 * v
    return h + h * jnp.tanh(h)


def _times_silu_of_half(u, h):
    uh = u * h
    return uh + uh * jnp.tanh(h)


def _params(sem, vmem=VMEM_LIMIT):
    return pltpu.CompilerParams(dimension_semantics=sem, vmem_limit_bytes=vmem)


def _mod_kernel(c_ref, w_ref, b_ref, o_ref):
    s = _silu(c_ref[...])
    o_ref[...] = jnp.dot(s, w_ref[...], preferred_element_type=F32, precision=lax.Precision.HIGHEST) + b_ref[...]


def _modulation(c_all, w_ada, b_ada):
    nb = c_all.shape[0]
    return pl.pallas_call(
        _mod_kernel,
        grid=(DEPTH, 3),
        in_specs=[
            pl.BlockSpec((nb, D_MODEL), lambda l, j: (0, 0)),
            pl.BlockSpec((None, D_MODEL, D_MODEL), lambda l, j: (l, 0, j)),
            pl.BlockSpec((None, 1, D_MODEL), lambda l, j: (l, 0, j)),
        ],
        out_specs=pl.BlockSpec((None, nb, D_MODEL), lambda l, j: (l, 0, j)),
        out_shape=jax.ShapeDtypeStruct((DEPTH, nb, 3 * D_MODEL), F32),
        compiler_params=_params(("parallel", "parallel")),
        name="adaln_mod",
    )(c_all, w_ada, b_ada.reshape(DEPTH, 1, 3 * D_MODEL))


def _rope_tables(seq):
    pos = jnp.arange(seq, dtype=F32)
    fa = jnp.exp(-math.log(ATT_THETA) * jnp.arange(A_ROT_HALF, dtype=F32) / A_ROT_HALF)
    ang = pos[:, None] * fa[None, :]
    cos, sin = jnp.cos(ang), jnp.sin(ang)
    rest = A_HEAD_DIM - 2 * A_ROT_HALF
    one = jnp.ones((seq, rest), F32)
    zero = jnp.zeros((seq, rest), F32)
    z8 = jnp.zeros((seq, A_ROT_HALF), F32)
    att_c = jnp.tile(jnp.concatenate([cos, cos, one], 1), (1, 2))
    att_a = jnp.tile(jnp.concatenate([-sin, z8, zero], 1), (1, 2))
    att_b = jnp.tile(jnp.concatenate([z8, sin, zero], 1), (1, 2))
    half = R_KEY_DIM // 2
    fr = jnp.exp(-math.log(RET_THETA) * jnp.arange(half, dtype=F32) / half)
    ang = pos[:, None] * fr[None, :]
    cos, sin = jnp.cos(ang), jnp.sin(ang)
    ret_c = jnp.concatenate([cos, cos], 1)
    ret_s = jnp.concatenate([-sin, sin], 1)
    return jnp.concatenate([att_c, att_a, att_b, ret_c, ret_s], 1)


def _inproj_kernel(x_ref, mod_ref, g_ref, w_ref, gq_ref, gk_ref, e_ref, tab_ref, o_ref, a4_ref, a16_ref, rows_scr):
    x = x_ref[...]
    ms = jnp.mean(x * x, axis=-1, keepdims=True)
    y = x * lax.rsqrt(ms + EPS) * g_ref[...]
    h = (y * (1.0 + mod_ref[1:2, :]) + mod_ref[0:1, :]).astype(BF16)

    att_c = tab_ref[:, 0 * LANES:1 * LANES]
    att_a = tab_ref[:, 1 * LANES:2 * LANES]
    att_b = tab_ref[:, 2 * LANES:3 * LANES]
    ret_c = tab_ref[:, 3 * LANES:4 * LANES]
    ret_s = tab_ref[:, 4 * LANES:5 * LANES]

    def pack_pair(a, b):
        abits = lax.bitcast_convert_type(a.astype(BF16).astype(F32), jnp.uint32)
        bbits = lax.bitcast_convert_type(b.astype(BF16).astype(F32), jnp.uint32)
        return abits | (bbits >> 16)

    def unpack_pair(w):
        hi = lax.bitcast_convert_type(w & jnp.uint32(0xFFFF0000), F32)
        lo = lax.bitcast_convert_type(w << 16, F32)
        return hi.astype(BF16), lo.astype(BF16)

    def emit_qkv(j, pair, val_a, val_b):
        lo_a = j * COL + 2 * pair * LANES
        lo_b = lo_a + LANES
        o_ref[:, lo_a:lo_a + LANES] = val_a.astype(BF16)
        o_ref[:, lo_b:lo_b + LANES] = val_b.astype(BF16)
        rows_scr[0, pair] = pack_pair(val_a, val_b)
        n4 = TM_IN // 4
        n16 = TM_IN // 16
        for r4 in range(4):
            by4 = rows_scr[0, pair, pl.ds(r4, n4, stride=4), :]
            rows_scr[1, pair, r4 * n4:(r4 + 1) * n4, :] = by4
            a4_ref[r4, :, lo_a:lo_a + LANES], a4_ref[r4, :, lo_b:lo_b + LANES] = unpack_pair(by4)
        for r4 in range(4):
            for r in range(4):
                by16 = rows_scr[1, pair, pl.ds(r4 * n4 + r, n16, stride=4), :]
                res = 4 * r + r4
                a16_ref[res, :, lo_a:lo_a + LANES], a16_ref[res, :, lo_b:lo_b + LANES] = unpack_pair(by16)

    for j in range(N_COL):
        acc = jnp.dot(h, w_ref[:, j * COL:(j + 1) * COL], preferred_element_type=F32)
        if j in (CB_QA, CB_KA):
            sq = (acc * acc).astype(BF16)
            ss = jnp.concatenate(
                [jnp.dot(sq[:, s * E_BLK:(s + 1) * E_BLK], e_ref[...], preferred_element_type=F32)
                 for s in range(COL // E_BLK)], axis=-1)
            gain = gq_ref[...] if j == CB_QA else gk_ref[...]
            yn = acc * lax.rsqrt(ss * (1.0 / A_HEAD_DIM) + EPS) * gain
            rots = []
            for c in range(N_CHUNK):
                xc = yn[:, c * LANES:(c + 1) * LANES]
                rots.append(xc * att_c + pltpu.roll(xc, LANES - A_ROT_HALF, 1) * att_a
                            + pltpu.roll(xc, A_ROT_HALF, 1) * att_b)
            for pair in range(N_CHUNK // 2):
                emit_qkv(j, pair, rots[2 * pair], rots[2 * pair + 1])
        elif j == CB_VA:
            for pair in range(N_CHUNK // 2):
                emit_qkv(j, pair, acc[:, 2 * pair * LANES:(2 * pair + 1) * LANES],
                         acc[:, (2 * pair + 1) * LANES:(2 * pair + 2) * LANES])
        elif j in (CB_QR, CB_KR):
            for c in range(N_CHUNK):
                xc = acc[:, c * LANES:(c + 1) * LANES]
                rot = xc * ret_c + pltpu.roll(xc, R_KEY_DIM // 2, 1) * ret_s
                if j == CB_KR:
                    rot = rot * (R_KEY_DIM ** -0.5)
                o_ref[:, j * COL + c * LANES:j * COL + (c + 1) * LANES] = rot.astype(BF16)
        else:
            o_ref[:, j * COL:(j + 1) * COL] = acc.astype(BF16)


def _inproj(x2, batch, seq, mod3, layer, b_off, norm_g, w_in, gq, gk, e_bd, tables):
    t = x2.shape[0]
    per_seq = seq // TM_IN
    const = dict(pipeline_mode=pl.Buffered(1))

    def dilated(dil):
        return pl.BlockSpec((None, dil, TM_IN // dil, QKV_W), lambda i: (i // per_seq, 0, i % per_seq, 0))

    return pl.pallas_call(
        _inproj_kernel,
        grid=(t // TM_IN,),
        in_specs=[
            pl.BlockSpec((TM_IN, D_MODEL), lambda i: (i, 0)),
            pl.BlockSpec((None, None, 3, D_MODEL), lambda i: (layer, i // per_seq + b_off, 0, 0)),
            pl.BlockSpec((1, D_MODEL), lambda i: (0, 0)),
            pl.BlockSpec((D_MODEL, IN_WIDTH), lambda i: (0, 0), **const),
            pl.BlockSpec((1, COL), lambda i: (0, 0)),
            pl.BlockSpec((1, COL), lambda i: (0, 0)),
            pl.BlockSpec((E_BLK, E_BLK), lambda i: (0, 0), **const),
            pl.BlockSpec((TM_IN, ROPE_TABLE_W), lambda i: (i % per_seq, 0)),
        ],
        out_specs=[pl.BlockSpec((TM_IN, IN_WIDTH), lambda i: (i, 0)), dilated(4), dilated(16)],
        out_shape=[
            jax.ShapeDtypeStruct((t, IN_WIDTH), BF16),
            jax.ShapeDtypeStruct((batch, 4, seq // 4, QKV_W), BF16),
            jax.ShapeDtypeStruct((batch, 16, seq // 16, QKV_W), BF16),
        ],
        scratch_shapes=[pltpu.VMEM((2, N_CHUNK // 2, TM_IN, LANES), jnp.uint32)],
        compiler_params=_params(("parallel",)),
        name="in_proj",
    )(x2, mod3, norm_g, w_in, gq, gk, e_bd, tables)


def _attn_kernel(q_ref, kp_ref, kc_ref, kn_ref, vp_ref, vc_ref, vn_ref, o_ref, st_ref, kbuf, vbuf, *, tq, nr, length):
    kbuf[:, 0:HALF_WIN, :] = kp_ref[...]
    kbuf[:, HALF_WIN:HALF_WIN + tq, :] = kc_ref[...]
    kbuf[:, HALF_WIN + tq:, :] = kn_ref[...]
    key_lane = lax.broadcasted_iota(jnp.int32, (nr, tq + 2 * HALF_WIN, LANES), 2)
    for p in range(A_WIDTH // LANES):
        src = slice(p * LANES, (p + 1) * LANES)
        dst = slice(2 * p * LANES, (2 * p + 1) * LANES)
        vbuf[:, 0:HALF_WIN, dst] = vp_ref[:, :, src]
        vbuf[:, HALF_WIN:HALF_WIN + tq, dst] = vc_ref[:, :, src]
        vbuf[:, HALF_WIN + tq:, dst] = vn_ref[:, :, src]
        den_lanes = (key_lane == A_HEADS + 2 * p) | (key_lane == A_HEADS + 2 * p + 1)
        vbuf[:, :, (2 * p + 1) * LANES:(2 * p + 2) * LANES] = jnp.where(den_lanes, 1.0, 0.0).astype(BF16)
    q0 = pl.program_id(2) * tq

    kj = lax.broadcasted_iota(jnp.int32, (Q_SUB, K_WIN), 1)
    qi = lax.broadcasted_iota(jnp.int32, (Q_SUB, K_WIN), 0)
    band = (kj >= qi) & (kj <= qi + 2 * HALF_WIN)
    lane = lax.broadcasted_iota(jnp.int32, (Q_SUB, LANES), 1)
    first_head = lane < A_HEAD_DIM
    odd_lane = (lane & 1) == 1
    sub_per_res = tq // Q_SUB

    def sub_block(t, carry):
        ri = t // sub_per_res
        r0 = pl.multiple_of((t % sub_per_res) * Q_SUB, Q_SUB)
        kpos = kj + (q0 + r0 - HALF_WIN)
        bias = jnp.where(band & (kpos >= 0) & (kpos < length), 0.0, NEG)
        bias = jnp.concatenate([bias, bias], axis=0)
        stat = jnp.zeros((Q_SUB, LANES), F32)
        for p in range(A_WIDTH // LANES):
            cols = slice(p * LANES, (p + 1) * LANES)
            q2 = q_ref[ri, pl.ds(r0, Q_SUB), cols]
            k2 = kbuf[ri, pl.ds(r0, K_WIN), cols]
            v2 = vbuf[ri, pl.ds(r0, K_WIN), 2 * p * LANES:(2 * p + 2) * LANES]
            zero = jnp.zeros_like(q2)
            qs = jnp.concatenate([jnp.where(first_head, q2, zero), jnp.where(first_head, zero, q2)], axis=0)
            sc = lax.dot_general(qs, k2, (((1,), (1,)), ((), ())), preferred_element_type=F32) + bias
            m = jnp.max(sc, axis=-1, keepdims=True)
            pr = jnp.exp2(sc - m)
            pvd = jnp.dot(pr.astype(BF16), v2, preferred_element_type=F32)
            pv, den = pvd[:, :LANES], pvd[:, LANES:]
            stat = stat + jnp.where(odd_lane, den[Q_SUB:], den[:Q_SUB])
            stat = jnp.where(lane == 2 * p, m[:Q_SUB], stat)
            stat = jnp.where(lane == 2 * p + 1, m[Q_SUB:], stat)
            o_ref[ri, pl.ds(r0, Q_SUB), cols] = jnp.where(first_head, pv[:Q_SUB], pv[Q_SUB:]).astype(o_ref.dtype)
        st_ref[ri, pl.ds(r0, Q_SUB), :] = stat
        return carry

    lax.fori_loop(0, nr * sub_per_res, sub_block, 0, unroll=ATT_UNROLL)


def _attention(qkv, dil):
    b, _, length, _ = qkv.shape
    tq = min(TQ_ATT, length)
    nr = min(dil, TQ_ATT // tq)
    nq = length // tq
    halo_per_q = tq // HALF_WIN
    last_halo = length // HALF_WIN - 1

    def cur(cb):
        return pl.BlockSpec((None, nr, tq, COL), lambda bi, r, i: (bi, r, i, cb))

    def prev(cb):
        return pl.BlockSpec((None, nr, HALF_WIN, COL),
                            lambda bi, r, i: (bi, r, jnp.maximum(i * halo_per_q - 1, 0), cb))

    def nxt(cb):
        return pl.BlockSpec((None, nr, HALF_WIN, COL),
                            lambda bi, r, i: (bi, r, jnp.minimum((i + 1) * halo_per_q, last_halo), cb))

    return pl.pallas_call(
        functools.partial(_attn_kernel, tq=tq, nr=nr, length=length),
        grid=(b, dil // nr, nq),
        in_specs=[cur(CB_QA), prev(CB_KA), cur(CB_KA), nxt(CB_KA), prev(CB_VA), cur(CB_VA), nxt(CB_VA)],
        out_specs=[
            pl.BlockSpec((None, nr, tq, A_WIDTH), lambda bi, r, i: (bi, r, i, 0)),
            pl.BlockSpec((None, nr, tq, LANES), lambda bi, r, i: (bi, r, i, 0)),
        ],
        out_shape=[
            jax.ShapeDtypeStruct((b, dil, length, A_WIDTH), BF16),
            jax.ShapeDtypeStruct((b, dil, length, LANES), F32),
        ],
        scratch_shapes=[pltpu.VMEM((nr, tq + 2 * HALF_WIN, COL), BF16),
                        pltpu.VMEM((nr, tq + 2 * HALF_WIN, 2 * COL), BF16)],
        compiler_params=_params(("parallel", "parallel", "parallel")),
        name=f"window_attn_d{dil}",
    )(qkv, qkv, qkv, qkv, qkv, qkv, qkv)


def _log_sigmoid(v):
    return jnp.minimum(v, 0.0) - jnp.log1p(jnp.exp(-jnp.abs(v)))


def _decays(dl_ref, head):
    lg_f = _log_sigmoid(dl_ref[pl.ds(head, 1), :])
    lg_b = _log_sigmoid(dl_ref[pl.ds(head + R_HEADS, 1), :])
    return lg_f, lg_b


def _chunk_tables(lg_f, lg_b):
    c = RET_CHUNK
    row = lax.broadcasted_iota(jnp.int32, (c, c), 0).astype(F32)
    col = lax.broadcasted_iota(jnp.int32, (c, c), 1).astype(F32)
    rel = row - col
    intra = (jnp.where(rel >= 0, jnp.exp(jnp.maximum(rel, 0.0) * lg_f), 0.0)
             + jnp.where(rel <= 0, jnp.exp(jnp.maximum(-rel, 0.0) * lg_b), 0.0))
    kd_f = jnp.exp((c - 1 - row) * lg_f)
    kd_b = jnp.exp(row * lg_b)
    qd_f = jnp.exp((row + 1.0) * lg_f)
    qd_b = jnp.exp((c - row) * lg_b)
    cd_f = jnp.exp(c * lg_f)
    cd_b = jnp.exp(c * lg_b)
    return intra, kd_f, kd_b, qd_f, qd_b, cd_f, cd_b


def _wide(v):
    return jnp.concatenate([v, v], axis=-1)


def _delta(k_chunk, kd, v_chunk):
    kt = (k_chunk.astype(F32) * kd).T.astype(BF16)
    return jnp.dot(kt, v_chunk, preferred_element_type=F32)


def _ret_state_kernel(dl_ref, kf_ref, vf_ref, kb_ref, vb_ref, fin_ref, bin_ref, sf, sb):
    head = pl.program_id(0)

    @pl.when(pl.program_id(1) == 0)
    def _():
        sf[...] = jnp.zeros_like(sf)
        sb[...] = jnp.zeros_like(sb)

    fin_ref[...] = sf[...]
    bin_ref[...] = sb[...]
    lg_f, lg_b = _decays(dl_ref, head)
    _, kd_f, kd_b, _, _, cd_f, cd_b = _chunk_tables(lg_f, lg_b)
    cd_f, cd_b = _wide(cd_f), _wide(cd_b)
    n_chunks = RET_BLOCK // RET_CHUNK

    def scan(t, carry):
        rf = pl.ds(pl.multiple_of(t * RET_CHUNK, RET_CHUNK), RET_CHUNK)
        rb = pl.ds(pl.multiple_of((n_chunks - 1 - t) * RET_CHUNK, RET_CHUNK), RET_CHUNK)
        sf[...] = sf[...] * cd_f + _delta(kf_ref[rf, :], kd_f, vf_ref[rf, :])
        sb[...] = sb[...] * cd_b + _delta(kb_ref[rb, :], kd_b, vb_ref[rb, :])
        return carry

    lax.fori_loop(0, n_chunks, scan, 0, unroll=True)


def _retention_states(p3, dl_tile):
    nb = p3.shape[0]
    k_cb = (CB_KR * COL) // R_KEY_DIM
    v_cb = OFF_VR // R_VAL_DIM
    shape = jax.ShapeDtypeStruct((nb, R_HEADS, R_KEY_DIM, R_VAL_DIM), F32)
    return pl.pallas_call(
        _ret_state_kernel,
        grid=(R_HEADS, nb),
        in_specs=[
            pl.BlockSpec((2 * R_HEADS, LANES), lambda h, j: (0, 0)),
            pl.BlockSpec((None, RET_BLOCK, R_KEY_DIM), lambda h, j: (j, 0, k_cb + h)),
            pl.BlockSpec((None, RET_BLOCK, R_VAL_DIM), lambda h, j: (j, 0, v_cb + h)),
            pl.BlockSpec((None, RET_BLOCK, R_KEY_DIM), lambda h, j: (nb - 1 - j, 0, k_cb + h)),
            pl.BlockSpec((None, RET_BLOCK, R_VAL_DIM), lambda h, j: (nb - 1 - j, 0, v_cb + h)),
        ],
        out_specs=[
            pl.BlockSpec((None, None, R_KEY_DIM, R_VAL_DIM), lambda h, j: (j, h, 0, 0)),
            pl.BlockSpec((None, None, R_KEY_DIM, R_VAL_DIM), lambda h, j: (nb - 1 - j, h, 0, 0)),
        ],
        out_shape=[shape, shape],
        scratch_shapes=[pltpu.VMEM((R_KEY_DIM, R_VAL_DIM), F32), pltpu.VMEM((R_KEY_DIM, R_VAL_DIM), F32)],
        compiler_params=_params(("parallel", "arbitrary")),
        name="retention_states",
    )(dl_tile, p3, p3, p3, p3)


def _ret_kernel(*refs, has_state):
    if has_state:
        dl_ref, gn_ref, q_ref, k_ref, v_ref, z_ref, fin_ref, bin_ref, o_ref, sf, sb, s_all = refs
    else:
        dl_ref, gn_ref, q_ref, k_ref, v_ref, z_ref, o_ref, sf, sb, s_all = refs
    n_chunks = RET_BLOCK // RET_CHUNK

    for hh in range(RET_HEADS_PER_STEP):
        head = pl.program_id(1) * RET_HEADS_PER_STEP + hh
        kcols = slice(hh * R_KEY_DIM, (hh + 1) * R_KEY_DIM)
        vcols = slice(hh * R_VAL_DIM, (hh + 1) * R_VAL_DIM)
        lg_f, lg_b = _decays(dl_ref, head)
        intra, kd_f, kd_b, qd_f, qd_b, cd_f, cd_b = _chunk_tables(lg_f, lg_b)
        cd_f, cd_b = _wide(cd_f), _wide(cd_b)
        gn = gn_ref[:, vcols]
        if has_state:
            sf[hh] = fin_ref[hh]
            sb[hh] = bin_ref[hh]
        else:
            sf[hh] = jnp.zeros((R_KEY_DIM, R_VAL_DIM), F32)
            sb[hh] = jnp.zeros((R_KEY_DIM, R_VAL_DIM), F32)

        for t in range(n_chunks):
            tb = n_chunks - 1 - t
            rf = slice(t * RET_CHUNK, (t + 1) * RET_CHUNK)
            rb = slice(tb * RET_CHUNK, (tb + 1) * RET_CHUNK)
            state_f = sf[hh]
            s_all[hh, t, 0:R_KEY_DIM, :] = state_f.astype(BF16)
            sf[hh] = state_f * cd_f + _delta(k_ref[rf, kcols], kd_f, v_ref[rf, vcols])
            state_b = sb[hh]
            s_all[hh, tb, R_KEY_DIM:2 * R_KEY_DIM, :] = state_b.astype(BF16)
            sb[hh] = state_b * cd_b + _delta(k_ref[rb, kcols], kd_b, v_ref[rb, vcols])

        for n in range(n_chunks):
            rows = slice(n * RET_CHUNK, (n + 1) * RET_CHUNK)
            q, k, v = q_ref[rows, kcols], k_ref[rows, kcols], v_ref[rows, vcols]
            qf = q.astype(F32)
            qq = jnp.concatenate([(qf * qd_f).astype(BF16), (qf * qd_b).astype(BF16)], axis=-1)
            sc = lax.dot_general(q, k, (((1,), (1,)), ((), ())), preferred_element_type=F32) * intra
            y = (jnp.dot(sc.astype(BF16), v, preferred_element_type=F32)
                 + jnp.dot(qq, s_all[hh, n], preferred_element_type=F32))
            mu = jnp.mean(y, axis=-1, keepdims=True)
            yc = y - mu
            var = jnp.mean(yc * yc, axis=-1, keepdims=True)
            yn = yc * lax.rsqrt(var + EPS) * gn
            o_ref[rows, vcols] = _times_silu_of_half(yn, z_ref[rows, vcols].astype(F32)).astype(BF16)


def _retention(p3, dl_tile, gn, states):
    nb = p3.shape[0]
    hps = RET_HEADS_PER_STEP
    kw, vw = hps * R_KEY_DIM, hps * R_VAL_DIM
    q_cb = (CB_QR * COL) // kw
    k_cb = (CB_KR * COL) // kw
    v_cb = OFF_VR // vw
    z_cb = OFF_ZR // vw
    in_specs = [
        pl.BlockSpec((2 * R_HEADS, LANES), lambda b, h: (0, 0)),
        pl.BlockSpec((1, vw), lambda b, h: (0, h)),
        pl.BlockSpec((None, RET_BLOCK, kw), lambda b, h: (b, 0, q_cb + h)),
        pl.BlockSpec((None, RET_BLOCK, kw), lambda b, h: (b, 0, k_cb + h)),
        pl.BlockSpec((None, RET_BLOCK, vw), lambda b, h: (b, 0, v_cb + h)),
        pl.BlockSpec((None, RET_BLOCK, vw), lambda b, h: (b, 0, z_cb + h)),
    ]
    args = [dl_tile, gn, p3, p3, p3, p3]
    if states is not None:
        st = pl.BlockSpec((None, hps, R_KEY_DIM, R_VAL_DIM), lambda b, h: (b, h, 0, 0))
        in_specs += [st, st]
        args += list(states)
    n_chunks = RET_BLOCK // RET_CHUNK
    out = pl.pallas_call(
        functools.partial(_ret_kernel, has_state=states is not None),
        grid=(nb, R_HEADS // hps),
        in_specs=in_specs,
        out_specs=pl.BlockSpec((None, RET_BLOCK, vw), lambda b, h: (b, 0, h)),
        out_shape=jax.ShapeDtypeStruct((nb, RET_BLOCK, R_V_WIDTH), BF16),
        scratch_shapes=[
            pltpu.VMEM((hps, R_KEY_DIM, R_VAL_DIM), F32),
            pltpu.VMEM((hps, R_KEY_DIM, R_VAL_DIM), F32),
            pltpu.VMEM((hps, n_chunks, 2 * R_KEY_DIM, R_VAL_DIM), BF16),
        ],
        compiler_params=_params(("parallel", "parallel")),
        name="retention",
    )(*args)
    return out.reshape(nb * RET_BLOCK, R_V_WIDTH)


def _out_kernel(x_ref, mod_ref, o1_ref, o4_ref, o16_ref, l1_ref, l4_ref, l16_ref, za_ref, ur_ref, ga_ref, gr_ref,
                wa_ref, wb_ref, wo_ref, y_ref, o_scr, l_scr, yr_scr):
    tm = x_ref.shape[0]
    yr_scr[...] = jnp.dot(ur_ref[...], wb_ref[...], preferred_element_type=F32)
    for gi, (dil, o_ref, l_ref) in enumerate(((4, o4_ref, l4_ref), (16, o16_ref, l16_ref))):
        n = tm // dil
        for r in range(dil):
            l_scr[gi, pl.ds(r, n, stride=dil), :] = l_ref[r]
            for pair in range(N_CHUNK // 2):
                a = o_ref[r, :, 2 * pair * LANES:(2 * pair + 1) * LANES].astype(F32)
                b = o_ref[r, :, (2 * pair + 1) * LANES:(2 * pair + 2) * LANES].astype(F32)
                word = lax.bitcast_convert_type(a, jnp.uint32) | (lax.bitcast_convert_type(b, jnp.uint32) >> 16)
                o_scr[gi, pair, pl.ds(r, n, stride=dil), :] = word

    lane = lax.broadcasted_iota(jnp.int32, (OUT_SUB, LANES), 1)
    first_head = lane < A_HEAD_DIM
    for s in range(tm // OUT_SUB):
        rows = slice(s * OUT_SUB, (s + 1) * OUT_SUB)
        stats = (l1_ref[rows, :], l_scr[0, rows, :], l_scr[1, rows, :])
        top = jnp.maximum(jnp.maximum(stats[0], stats[1]), stats[2])
        es = [jnp.exp2(v - top) for v in stats]
        dens = [pltpu.roll(v, LANES - A_HEADS, 1) for v in stats]
        total = es[0] * dens[0] + es[1] * dens[1] + es[2] * dens[2]
        total = jnp.where(lane < A_HEADS, total, 1.0)
        ws = [e / total for e in es]
        ua = []
        for c in range(N_CHUNK):
            cols = slice(c * LANES, (c + 1) * LANES)
            words = (o_scr[0, c // 2, rows, :], o_scr[1, c // 2, rows, :])
            if c % 2 == 0:
                halves = [w & jnp.uint32(0xFFFF0000) for w in words]
            else:
                halves = [w << 16 for w in words]
            outs = (o1_ref[rows, cols].astype(F32),
                    lax.bitcast_convert_type(halves[0], F32), lax.bitcast_convert_type(halves[1], F32))
            ya = jnp.zeros((OUT_SUB, LANES), F32)
            for g in range(3):
                w_pair = jnp.where(first_head, ws[g][:, 2 * c:2 * c + 1], ws[g][:, 2 * c + 1:2 * c + 2])
                ya = ya + w_pair * outs[g]
            ua.append(_times_silu_of_half(ya, za_ref[rows, cols].astype(F32)).astype(BF16))
        ua = jnp.concatenate(ua, axis=-1)
        ya_p = jnp.dot(ua, wa_ref[...], preferred_element_type=F32)
        yr_p = yr_scr[rows, :]
        ta = jnp.tanh(ga_ref[rows, :].astype(F32))
        tr = jnp.tanh(gr_ref[rows, :].astype(F32))
        merged2 = (ya_p + ya_p * ta) + (yr_p + yr_p * tr)
        out = jnp.dot(merged2.astype(BF16), wo_ref[...], preferred_element_type=F32)
        y_ref[rows, :] = x_ref[rows, :] + (0.5 * mod_ref[2:3, :]) * out


def _output(x2, mod3, layer, b_off, seq, attn, p2, ur, wa, wb, wo):
    t = x2.shape[0]
    per_seq = seq // TM_OUT
    const = dict(pipeline_mode=pl.Buffered(1))
    (o1, l1), (o4, l4), (o16, l16) = attn

    def rows(width, cb=0):
        return pl.BlockSpec((TM_OUT, width), lambda i: (i, cb))

    def dilated(dil, width):
        return pl.BlockSpec((None, dil, TM_OUT // dil, width), lambda i: (i // per_seq, 0, i % per_seq, 0))

    return pl.pallas_call(
        _out_kernel,
        grid=(t // TM_OUT,),
        in_specs=[
            rows(D_MODEL),
            pl.BlockSpec((None, None, 3, D_MODEL), lambda i: (layer, i // per_seq + b_off, 0, 0)),
            rows(A_WIDTH), dilated(4, A_WIDTH), dilated(16, A_WIDTH),
            rows(LANES), dilated(4, LANES), dilated(16, LANES),
            rows(A_WIDTH, CB_ZA),
            rows(R_V_WIDTH),
            rows(D_MODEL, OFF_GA // D_MODEL),
            rows(D_MODEL, OFF_GR // D_MODEL),
            pl.BlockSpec((A_WIDTH, D_MODEL), lambda i: (0, 0), **const),
            pl.BlockSpec((R_V_WIDTH, D_MODEL), lambda i: (0, 0), **const),
            pl.BlockSpec((D_MODEL, D_MODEL), lambda i: (0, 0), **const),
        ],
        out_specs=rows(D_MODEL),
        out_shape=jax.ShapeDtypeStruct((t, D_MODEL), F32),
        scratch_shapes=[
            pltpu.VMEM((2, N_CHUNK // 2, TM_OUT, LANES), jnp.uint32),
            pltpu.VMEM((2, TM_OUT, LANES), F32),
            pltpu.VMEM((TM_OUT, D_MODEL), F32),
        ],
        compiler_params=_params(("parallel",), vmem=VMEM_LIMIT_OUT),
        name="merge_out_proj",
    )(x2, mod3, o1.reshape(t, A_WIDTH), o4, o16, l1.reshape(t, LANES), l4, l16, p2, ur, p2, p2, wa, wb, wo)


def _mixer_layer(x2, batch, seq, mod3, layer, b_off, lw, tables, e_bd):
    p2, a4, a16 = _inproj(x2, batch, seq, mod3, layer, b_off, lw["norm_g"], lw["w_in"], lw["gq"], lw["gk"], e_bd,
                          tables)
    attn = [_attention(qkv, dil) for qkv, dil in zip((p2.reshape(batch, 1, seq, IN_WIDTH), a4, a16), DILATIONS)]
    nb = batch * seq // RET_BLOCK
    pr = p2.reshape(nb, RET_BLOCK, IN_WIDTH)
    states = _retention_states(pr, lw["dl"]) if seq > RET_BLOCK else None
    ur = _retention(pr, lw["dl"], lw["gn"], states)
    return _output(x2, mod3, layer, b_off, seq, attn, p2, ur, lw["wa"], lw["wb"], lw["wo"])


def _half_gate_columns():
    col = jnp.arange(IN_WIDTH)
    za = (col >= CB_ZA * COL) & (col < (CB_ZA + 1) * COL)
    return jnp.where(za | (col >= OFF_ZR), 0.5, 1.0).astype(F32)[None, :]


def _layer_weights(layer,norm_g, w_in, q_norm_g, k_norm_g, ret_decay_logit, ret_norm_g, w_proj_a, w_proj_b, w_out):
    return {
        "norm_g": norm_g[layer].reshape(1, D_MODEL),
        "w_in": (w_in[layer] * _half_gate_columns()).astype(BF16),
        "gq": (jnp.tile(q_norm_g[layer], A_HEADS) * (A_HEAD_DIM ** -0.5 * LOG2E)).reshape(1, A_WIDTH),
        "gk": jnp.tile(k_norm_g[layer], A_HEADS).reshape(1, A_WIDTH),
        "dl": jnp.broadcast_to(ret_decay_logit[layer].astype(F32).reshape(2 * R_HEADS, 1), (2 * R_HEADS, LANES)),
        "gn": ret_norm_g[layer].reshape(1, R_V_WIDTH),
        "wa": w_proj_a[layer].astype(BF16),
        "wb": w_proj_b[layer].astype(BF16),
        "wo": w_out[layer].astype(BF16),
    }


def kernel(x_prompt, x_sample, c_prompt, c_sample, norm_g, w_ada, b_ada, w_in, q_norm_g, k_norm_g, ret_decay_logit,
           ret_norm_g, w_proj_a, w_proj_b, w_out):
    bp, sp, _ = x_prompt.shape
    bs, ss, _ = x_sample.shape
    pad = (-(bp + bs)) % 8
    c_all = jnp.concatenate([c_prompt, c_sample, jnp.zeros((pad, D_MODEL), F32)], axis=0)
    mod3 = _modulation(c_all, w_ada, b_ada).reshape(DEPTH, bp + bs + pad, 3, D_MODEL)
    head_of_lane = jnp.arange(E_BLK) // A_HEAD_DIM
    e_bd = (head_of_lane[:, None] == head_of_lane[None, :]).astype(BF16)
    tab_p = _rope_tables(sp)
    tab_s = tab_p if ss == sp else _rope_tables(ss)
    yp = x_prompt.reshape(bp * sp, D_MODEL)
    ys = x_sample.reshape(bs * ss, D_MODEL)
    for layer in range(DEPTH):
        lw = _layer_weights(layer, norm_g, w_in, q_norm_g, k_norm_g, ret_decay_logit, ret_norm_g, w_proj_a, w_proj_b,
                            w_out)
        yp = _mixer_layer(yp, bp, sp, mod3, layer, 0, lw, tab_p, e_bd)
        ys = _mixer_layer(ys, bs, ss, mod3, layer, bp, lw, tab_s, e_bd)
    return yp.reshape(bp, sp, D_MODEL), ys.reshape(bs, ss, D_MODEL)
```

```python
import functools
import math

import jax
import jax.numpy as jnp
from jax import lax
from jax.experimental import pallas as pl
from jax.experimental.pallas import tpu as pltpu

D_MODEL = 1024
DEPTH = 4
A_HEADS = 8
A_HEAD_DIM = 64
A_WIDTH = A_HEADS * A_HEAD_DIM
A_ROT_HALF = A_HEAD_DIM // 8
ATT_THETA = 500000.0
DILATIONS = (1, 4, 16)
HALF_WIN = 64
R_HEADS = 4
R_KEY_DIM = 128
R_VAL_DIM = 256
R_V_WIDTH = R_HEADS * R_VAL_DIM
RET_THETA = 10000.0
RET_CHUNK = 128
IN_WIDTH = 4 * A_WIDTH + 2 * R_HEADS * R_KEY_DIM + 2 * R_V_WIDTH + 2 * D_MODEL
EPS = 1e-6
NEG = -1e30
LOG2E = 1.4426950408889634

LANES = 128
COL = 512
N_COL = IN_WIDTH // COL
N_CHUNK = COL // LANES
E_BLK = 256
CB_QA, CB_KA, CB_VA, CB_ZA, CB_QR, CB_KR = 0, 1, 2, 3, 4, 5
OFF_VR, OFF_ZR, OFF_GA, OFF_GR = 3072, 4096, 5120, 6144
QKV_W = 3 * A_WIDTH
ROPE_TABLE_W = 5 * LANES

TM_IN = 512
TM_OUT = 1024
OUT_SUB = 256
TQ_ATT = 2048
Q_SUB = 128
K_WIN = Q_SUB + 2 * HALF_WIN
ATT_UNROLL = 16
RET_BLOCK = 2048
RET_HEADS_PER_STEP = 2
VMEM_LIMIT = 56 * 1024 * 1024
VMEM_LIMIT_OUT = 60 * 1024 * 1024

F32 = jnp.float32
BF16 = jnp.bfloat16


def _silu(v):
    h = 0.5 * v
    return h + h * jnp.tanh(h)


def _times_silu_of_half(u, h):
    uh = u * h
    return uh + uh * jnp.tanh(h)


def _params(sem, vmem=VMEM_LIMIT):
    return pltpu.CompilerParams(dimension_semantics=sem, vmem_limit_bytes=vmem)


def _mod_kernel(c_ref, w_ref, b_ref, o_ref):
    s = _silu(c_ref[...])
    o_ref[...] = jnp.dot(s, w_ref[...], preferred_element_type=F32, precision=lax.Precision.HIGHEST) + b_ref[...]


def _modulation(c_all, w_ada, b_ada):
    nb = c_all.shape[0]
    return pl.pallas_call(
        _mod_kernel,
        grid=(DEPTH, 3),
        in_specs=[
            pl.BlockSpec((nb, D_MODEL), lambda l, j: (0, 0)),
            pl.BlockSpec((None, D_MODEL, D_MODEL), lambda l, j: (l, 0, j)),
            pl.BlockSpec((None, 1, D_MODEL), lambda l, j: (l, 0, j)),
        ],
        out_specs=pl.BlockSpec((None, nb, D_MODEL), lambda l, j: (l, 0, j)),
        out_shape=jax.ShapeDtypeStruct((DEPTH, nb, 3 * D_MODEL), F32),
        compiler_params=_params(("parallel", "parallel")),
        name="adaln_mod",
    )(c_all, w_ada, b_ada.reshape(DEPTH, 1, 3 * D_MODEL))


def _rope_tables(seq):
    pos = jnp.arange(seq, dtype=F32)
    fa = jnp.exp(-math.log(ATT_THETA) * jnp.arange(A_ROT_HALF, dtype=F32) / A_ROT_HALF)
    ang = pos[:, None] * fa[None, :]
    cos, sin = jnp.cos(ang), jnp.sin(ang)
    rest = A_HEAD_DIM - 2 * A_ROT_HALF
    one = jnp.ones((seq, rest), F32)
    zero = jnp.zeros((seq, rest), F32)
    z8 = jnp.zeros((seq, A_ROT_HALF), F32)
    att_c = jnp.tile(jnp.concatenate([cos, cos, one], 1), (1, 2))
    att_a = jnp.tile(jnp.concatenate([-sin, z8, zero], 1), (1, 2))
    att_b = jnp.tile(jnp.concatenate([z8, sin, zero], 1), (1, 2))
    half = R_KEY_DIM // 2
    fr = jnp.exp(-math.log(RET_THETA) * jnp.arange(half, dtype=F32) / half)
    ang = pos[:, None] * fr[None, :]
    cos, sin = jnp.cos(ang), jnp.sin(ang)
    ret_c = jnp.concatenate([cos, cos], 1)
    ret_s = jnp.concatenate([-sin, sin], 1)
    return jnp.concatenate([att_c, att_a, att_b, ret_c, ret_s], 1)


def _inproj_kernel(x_ref, mod_ref, g_ref, w_ref, gq_ref, gk_ref, e_ref, tab_ref, o_ref, a4_ref, a16_ref, rows_scr):
    x = x_ref[...]
    ms = jnp.mean(x * x, axis=-1, keepdims=True)
    y = x * lax.rsqrt(ms + EPS) * g_ref[...]
    h = (y * (1.0 + mod_ref[1:2, :]) + mod_ref[0:1, :]).astype(BF16)

    att_c = tab_ref[:, 0 * LANES:1 * LANES]
    att_a = tab_ref[:, 1 * LANES:2 * LANES]
    att_b = tab_ref[:, 2 * LANES:3 * LANES]
    ret_c = tab_ref[:, 3 * LANES:4 * LANES]
    ret_s = tab_ref[:, 4 * LANES:5 * LANES]

    def pack_pair(a, b):
        abits = lax.bitcast_convert_type(a.astype(BF16).astype(F32), jnp.uint32)
        bbits = lax.bitcast_convert_type(b.astype(BF16).astype(F32), jnp.uint32)
        return abits | (bbits >> 16)

    def unpack_pair(w):
        hi = lax.bitcast_convert_type(w & jnp.uint32(0xFFFF0000), F32)
        lo = lax.bitcast_convert_type(w << 16, F32)
        return hi.astype(BF16), lo.astype(BF16)

    def emit_qkv(j, pair, val_a, val_b):
        lo_a = j * COL + 2 * pair * LANES
        lo_b = lo_a + LANES
        o_ref[:, lo_a:lo_a + LANES] = val_a.astype(BF16)
        o_ref[:, lo_b:lo_b + LANES] = val_b.astype(BF16)
        rows_scr[0, pair] = pack_pair(val_a, val_b)
        n4 = TM_IN // 4
        n16 = TM_IN // 16
        for r4 in range(4):
            by4 = rows_scr[0, pair, pl.ds(r4, n4, stride=4), :]
            rows_scr[1, pair, r4 * n4:(r4 + 1) * n4, :] = by4
            a4_ref[r4, :, lo_a:lo_a + LANES], a4_ref[r4, :, lo_b:lo_b + LANES] = unpack_pair(by4)
        for r4 in range(4):
            for r in range(4):
                by16 = rows_scr[1, pair, pl.ds(r4 * n4 + r, n16, stride=4), :]
                res = 4 * r + r4
                a16_ref[res, :, lo_a:lo_a + LANES], a16_ref[res, :, lo_b:lo_b + LANES] = unpack_pair(by16)

    for j in range(N_COL):
        acc = jnp.dot(h, w_ref[:, j * COL:(j + 1) * COL], preferred_element_type=F32)
        if j in (CB_QA, CB_KA):
            sq = (acc * acc).astype(BF16)
            ss = jnp.concatenate(
                [jnp.dot(sq[:, s * E_BLK:(s + 1) * E_BLK], e_ref[...], preferred_element_type=F32)
                 for s in range(COL // E_BLK)], axis=-1)
            gain = gq_ref[...] if j == CB_QA else gk_ref[...]
            yn = acc * lax.rsqrt(ss * (1.0 / A_HEAD_DIM) + EPS) * gain
            rots = []
            for c in range(N_CHUNK):
                xc = yn[:, c * LANES:(c + 1) * LANES]
                rots.append(xc * att_c + pltpu.roll(xc, LANES - A_ROT_HALF, 1) * att_a
                            + pltpu.roll(xc, A_ROT_HALF, 1) * att_b)
            for pair in range(N_CHUNK // 2):
                emit_qkv(j, pair, rots[2 * pair], rots[2 * pair + 1])
        elif j == CB_VA:
            for pair in range(N_CHUNK // 2):
                emit_qkv(j, pair, acc[:, 2 * pair * LANES:(2 * pair + 1) * LANES],
                         acc[:, (2 * pair + 1) * LANES:(2 * pair + 2) * LANES])
        elif j in (CB_QR, CB_KR):
            for c in range(N_CHUNK):
                xc = acc[:, c * LANES:(c + 1) * LANES]
                rot = xc * ret_c + pltpu.roll(xc, R_KEY_DIM // 2, 1) * ret_s
                if j == CB_KR:
                    rot = rot * (R_KEY_DIM ** -0.5)
                o_ref[:, j * COL + c * LANES:j * COL + (c + 1) * LANES] = rot.astype(BF16)
        else:
            o_ref[:, j * COL:(j + 1) * COL] = acc.astype(BF16)


def _inproj(x2, batch, seq, mod3, layer, b_off, norm_g, w_in, gq, gk, e_bd, tables):
    t = x2.shape[0]
    per_seq = seq // TM_IN
    const = dict(pipeline_mode=pl.Buffered(1))

    def dilated(dil):
        return pl.BlockSpec((None, dil, TM_IN // dil, QKV_W), lambda i: (i // per_seq, 0, i % per_seq, 0))

    return pl.pallas_call(
        _inproj_kernel,
        grid=(t // TM_IN,),
        in_specs=[
            pl.BlockSpec((TM_IN, D_MODEL), lambda i: (i, 0)),
            pl.BlockSpec((None, None, 3, D_MODEL), lambda i: (layer, i // per_seq + b_off, 0, 0)),
            pl.BlockSpec((1, D_MODEL), lambda i: (0, 0)),
            pl.BlockSpec((D_MODEL, IN_WIDTH), lambda i: (0, 0), **const),
            pl.BlockSpec((1, COL), lambda i: (0, 0)),
            pl.BlockSpec((1, COL), lambda i: (0, 0)),
            pl.BlockSpec((E_BLK, E_BLK), lambda i: (0, 0), **const),
            pl.BlockSpec((TM_IN, ROPE_TABLE_W), lambda i: (i % per_seq, 0)),
        ],
        out_specs=[pl.BlockSpec((TM_IN, IN_WIDTH), lambda i: (i, 0)), dilated(4), dilated(16)],
        out_shape=[
            jax.ShapeDtypeStruct((t, IN_WIDTH), BF16),
            jax.ShapeDtypeStruct((batch, 4, seq // 4, QKV_W), BF16),
            jax.ShapeDtypeStruct((batch, 16, seq // 16, QKV_W), BF16),
        ],
        scratch_shapes=[pltpu.VMEM((2, N_CHUNK // 2, TM_IN, LANES), jnp.uint32)],
        compiler_params=_params(("parallel",)),
        name="in_proj",
    )(x2, mod3, norm_g, w_in, gq, gk, e_bd, tables)


def _attn_kernel(q_ref, kp_ref, kc_ref, kn_ref, vp_ref, vc_ref, vn_ref, o_ref, st_ref, kbuf, vbuf, *, tq, nr, length):
    kbuf[:, 0:HALF_WIN, :] = kp_ref[...]
    kbuf[:, HALF_WIN:HALF_WIN + tq, :] = kc_ref[...]
    kbuf[:, HALF_WIN + tq:, :] = kn_ref[...]
    key_lane = lax.broadcasted_iota(jnp.int32, (nr, tq + 2 * HALF_WIN, LANES), 2)
    for p in range(A_WIDTH // LANES):
        src = slice(p * LANES, (p + 1) * LANES)
        dst = slice(2 * p * LANES, (2 * p + 1) * LANES)
        vbuf[:, 0:HALF_WIN, dst] = vp_ref[:, :, src]
        vbuf[:, HALF_WIN:HALF_WIN + tq, dst] = vc_ref[:, :, src]
        vbuf[:, HALF_WIN + tq:, dst] = vn_ref[:, :, src]
        den_lanes = (key_lane == A_HEADS + 2 * p) | (key_lane == A_HEADS + 2 * p + 1)
        vbuf[:, :, (2 * p + 1) * LANES:(2 * p + 2) * LANES] = jnp.where(den_lanes, 1.0, 0.0).astype(BF16)
    q0 = pl.program_id(2) * tq

    kj = lax.broadcasted_iota(jnp.int32, (Q_SUB, K_WIN), 1)
    qi = lax.broadcasted_iota(jnp.int32, (Q_SUB, K_WIN), 0)
    band = (kj >= qi) & (kj <= qi + 2 * HALF_WIN)
    lane = lax.broadcasted_iota(jnp.int32, (Q_SUB, LANES), 1)
    first_head = lane < A_HEAD_DIM
    odd_lane = (lane & 1) == 1
    sub_per_res = tq // Q_SUB

    def sub_block(t, carry):
        ri = t // sub_per_res
        r0 = pl.multiple_of((t % sub_per_res) * Q_SUB, Q_SUB)
        kpos = kj + (q0 + r0 - HALF_WIN)
        bias = jnp.where(band & (kpos >= 0) & (kpos < length), 0.0, NEG)
        bias = jnp.concatenate([bias, bias], axis=0)
        stat = jnp.zeros((Q_SUB, LANES), F32)
        for p in range(A_WIDTH // LANES):
            cols = slice(p * LANES, (p + 1) * LANES)
            q2 = q_ref[ri, pl.ds(r0, Q_SUB), cols]
            k2 = kbuf[ri, pl.ds(r0, K_WIN), cols]
            v2 = vbuf[ri, pl.ds(r0, K_WIN), 2 * p * LANES:(2 * p + 2) * LANES]
            zero = jnp.zeros_like(q2)
            qs = jnp.concatenate([jnp.where(first_head, q2, zero), jnp.where(first_head, zero, q2)], axis=0)
            sc = lax.dot_general(qs, k2, (((1,), (1,)), ((), ())), preferred_element_type=F32) + bias
            m = jnp.max(sc, axis=-1, keepdims=True)
            pr = jnp.exp2(sc - m)
            pvd = jnp.dot(pr.astype(BF16), v2, preferred_element_type=F32)
            pv, den = pvd[:, :LANES], pvd[:, LANES:]
            stat = stat + jnp.where(odd_lane, den[Q_SUB:], den[:Q_SUB])
            stat = jnp.where(lane == 2 * p, m[:Q_SUB], stat)
            stat = jnp.where(lane == 2 * p + 1, m[Q_SUB:], stat)
            o_ref[ri, pl.ds(r0, Q_SUB), cols] = jnp.where(first_head, pv[:Q_SUB], pv[Q_SUB:]).astype(o_ref.dtype)
        st_ref[ri, pl.ds(r0, Q_SUB), :] = stat
        return carry

    lax.fori_loop(0, nr * sub_per_res, sub_block, 0, unroll=ATT_UNROLL)


def _attention(qkv, dil):
    b, _, length, _ = qkv.shape
    tq = min(TQ_ATT, length)
    nr = min(dil, TQ_ATT // tq)
    nq = length // tq
    halo_per_q = tq // HALF_WIN
    last_halo = length // HALF_WIN - 1

    def cur(cb):
        return pl.BlockSpec((None, nr, tq, COL), lambda bi, r, i: (bi, r, i, cb))

    def prev(cb):
        return pl.BlockSpec((None, nr, HALF_WIN, COL),
                            lambda bi, r, i: (bi, r, jnp.maximum(i * halo_per_q - 1, 0), cb))

    def nxt(cb):
        return pl.BlockSpec((None, nr, HALF_WIN, COL),
                            lambda bi, r, i: (bi, r, jnp.minimum((i + 1) * halo_per_q, last_halo), cb))

    return pl.pallas_call(
        functools.partial(_attn_kernel, tq=tq, nr=nr, length=length),
        grid=(b, dil // nr, nq),
        in_specs=[cur(CB_QA), prev(CB_KA), cur(CB_KA), nxt(CB_KA), prev(CB_VA), cur(CB_VA), nxt(CB_VA)],
        out_specs=[
            pl.BlockSpec((None, nr, tq, A_WIDTH), lambda bi, r, i: (bi, r, i, 0)),
            pl.BlockSpec((None, nr, tq, LANES), lambda bi, r, i: (bi, r, i, 0)),
        ],
        out_shape=[
            jax.ShapeDtypeStruct((b, dil, length, A_WIDTH), BF16),
            jax.ShapeDtypeStruct((b, dil, length, LANES), F32),
        ],
        scratch_shapes=[pltpu.VMEM((nr, tq + 2 * HALF_WIN, COL), BF16),
                        pltpu.VMEM((nr, tq + 2 * HALF_WIN, 2 * COL), BF16)],
        compiler_params=_params(("parallel", "parallel", "parallel")),
        name=f"window_attn_d{dil}",
    )(qkv, qkv, qkv, qkv, qkv, qkv, qkv)


def _log_sigmoid(v):
    return jnp.minimum(v, 0.0) - jnp.log1p(jnp.exp(-jnp.abs(v)))


def _decays(dl_ref, head):
    lg_f = _log_sigmoid(dl_ref[pl.ds(head, 1), :])
    lg_b = _log_sigmoid(dl_ref[pl.ds(head + R_HEADS, 1), :])
    return lg_f, lg_b


def _chunk_tables(lg_f, lg_b):
    c = RET_CHUNK
    row = lax.broadcasted_iota(jnp.int32, (c, c), 0).astype(F32)
    col = lax.broadcasted_iota(jnp.int32, (c, c), 1).astype(F32)
    rel = row - col
    intra = (jnp.where(rel >= 0, jnp.exp(jnp.maximum(rel, 0.0) * lg_f), 0.0)
             + jnp.where(rel <= 0, jnp.exp(jnp.maximum(-rel, 0.0) * lg_b), 0.0))
    kd_f = jnp.exp((c - 1 - row) * lg_f)
    kd_b = jnp.exp(row * lg_b)
    qd_f = jnp.exp((row + 1.0) * lg_f)
    qd_b = jnp.exp((c - row) * lg_b)
    cd_f = jnp.exp(c * lg_f)
    cd_b = jnp.exp(c * lg_b)
    return intra, kd_f, kd_b, qd_f, qd_b, cd_f, cd_b


def _wide(v):
    return jnp.concatenate([v, v], axis=-1)


def _delta(k_chunk, kd, v_chunk):
    kt = (k_chunk.astype(F32) * kd).T.astype(BF16)
    return jnp.dot(kt, v_chunk, preferred_element_type=F32)


def _ret_state_kernel(dl_ref, kf_ref, vf_ref, kb_ref, vb_ref, fin_ref, bin_ref, sf, sb):
    head = pl.program_id(0)

    @pl.when(pl.program_id(1) == 0)
    def _():
        sf[...] = jnp.zeros_like(sf)
        sb[...] = jnp.zeros_like(sb)

    fin_ref[...] = sf[...]
    bin_ref[...] = sb[...]
    lg_f, lg_b = _decays(dl_ref, head)
    _, kd_f, kd_b, _, _, cd_f, cd_b = _chunk_tables(lg_f, lg_b)
    cd_f, cd_b = _wide(cd_f), _wide(cd_b)
    n_chunks = RET_BLOCK // RET_CHUNK

    def scan(t, carry):
        rf = pl.ds(pl.multiple_of(t * RET_CHUNK, RET_CHUNK), RET_CHUNK)
        rb = pl.ds(pl.multiple_of((n_chunks - 1 - t) * RET_CHUNK, RET_CHUNK), RET_CHUNK)
        sf[...] = sf[...] * cd_f + _delta(kf_ref[rf, :], kd_f, vf_ref[rf, :])
        sb[...] = sb[...] * cd_b + _delta(kb_ref[rb, :], kd_b, vb_ref[rb, :])
        return carry

    lax.fori_loop(0, n_chunks, scan, 0, unroll=True)


def _retention_states(p3, dl_tile):
    nb = p3.shape[0]
    k_cb = (CB_KR * COL) // R_KEY_DIM
    v_cb = OFF_VR // R_VAL_DIM
    shape = jax.ShapeDtypeStruct((nb, R_HEADS, R_KEY_DIM, R_VAL_DIM), F32)
    return pl.pallas_call(
        _ret_state_kernel,
        grid=(R_HEADS, nb),
        in_specs=[
            pl.BlockSpec((2 * R_HEADS, LANES), lambda h, j: (0, 0)),
            pl.BlockSpec((None, RET_BLOCK, R_KEY_DIM), lambda h, j: (j, 0, k_cb + h)),
            pl.BlockSpec((None, RET_BLOCK, R_VAL_DIM), lambda h, j: (j, 0, v_cb + h)),
            pl.BlockSpec((None, RET_BLOCK, R_KEY_DIM), lambda h, j: (nb - 1 - j, 0, k_cb + h)),
            pl.BlockSpec((None, RET_BLOCK, R_VAL_DIM), lambda h, j: (nb - 1 - j, 0, v_cb + h)),
        ],
        out_specs=[
            pl.BlockSpec((None, None, R_KEY_DIM, R_VAL_DIM), lambda h, j: (j, h, 0, 0)),
            pl.BlockSpec((None, None, R_KEY_DIM, R_VAL_DIM), lambda h, j: (nb - 1 - j, h, 0, 0)),
        ],
        out_shape=[shape, shape],
        scratch_shapes=[pltpu.VMEM((R_KEY_DIM, R_VAL_DIM), F32), pltpu.VMEM((R_KEY_DIM, R_VAL_DIM), F32)],
        compiler_params=_params(("parallel", "arbitrary")),
        name="retention_states",
    )(dl_tile, p3, p3, p3, p3)


def _ret_kernel(*refs, has_state):
    if has_state:
        dl_ref, gn_ref, q_ref, k_ref, v_ref, z_ref, fin_ref, bin_ref, o_ref, sf, sb, s_all = refs
    else:
        dl_ref, gn_ref, q_ref, k_ref, v_ref, z_ref, o_ref, sf, sb, s_all = refs
    n_chunks = RET_BLOCK // RET_CHUNK

    for hh in range(RET_HEADS_PER_STEP):
        head = pl.program_id(1) * RET_HEADS_PER_STEP + hh
        kcols = slice(hh * R_KEY_DIM, (hh + 1) * R_KEY_DIM)
        vcols = slice(hh * R_VAL_DIM, (hh + 1) * R_VAL_DIM)
        lg_f, lg_b = _decays(dl_ref, head)
        intra, kd_f, kd_b, qd_f, qd_b, cd_f, cd_b = _chunk_tables(lg_f, lg_b)
        cd_f, cd_b = _wide(cd_f), _wide(cd_b)
        gn = gn_ref[:, vcols]
        if has_state:
            sf[hh] = fin_ref[hh]
            sb[hh] = bin_ref[hh]
        else:
            sf[hh] = jnp.zeros((R_KEY_DIM, R_VAL_DIM), F32)
            sb[hh] = jnp.zeros((R_KEY_DIM, R_VAL_DIM), F32)

        for t in range(n_chunks):
            tb = n_chunks - 1 - t
            rf = slice(t * RET_CHUNK, (t + 1) * RET_CHUNK)
            rb = slice(tb * RET_CHUNK, (tb + 1) * RET_CHUNK)
            state_f = sf[hh]
            s_all[hh, t, 0:R_KEY_DIM, :] = state_f.astype(BF16)
            sf[hh] = state_f * cd_f + _delta(k_ref[rf, kcols], kd_f, v_ref[rf, vcols])
            state_b = sb[hh]
            s_all[hh, tb, R_KEY_DIM:2 * R_KEY_DIM, :] = state_b.astype(BF16)
            sb[hh] = state_b * cd_b + _delta(k_ref[rb, kcols], kd_b, v_ref[rb, vcols])

        for n in range(n_chunks):
            rows = slice(n * RET_CHUNK, (n + 1) * RET_CHUNK)
            q, k, v = q_ref[rows, kcols], k_ref[rows, kcols], v_ref[rows, vcols]
            qf = q.astype(F32)
            qq = jnp.concatenate([(qf * qd_f).astype(BF16), (qf * qd_b).astype(BF16)], axis=-1)
            sc = lax.dot_general(q, k, (((1,), (1,)), ((), ())), preferred_element_type=F32) * intra
            y = (jnp.dot(sc.astype(BF16), v, preferred_element_type=F32)
                 + jnp.dot(qq, s_all[hh, n], preferred_element_type=F32))
            mu = jnp.mean(y, axis=-1, keepdims=True)
            yc = y - mu
            var = jnp.mean(yc * yc, axis=-1, keepdims=True)
            yn = yc * lax.rsqrt(var + EPS) * gn
            o_ref[rows, vcols] = _times_silu_of_half(yn, z_ref[rows, vcols].astype(F32)).astype(BF16)


def _retention(p3, dl_tile, gn, states):
    nb = p3.shape[0]
    hps = RET_HEADS_PER_STEP
    kw, vw = hps * R_KEY_DIM, hps * R_VAL_DIM
    q_cb = (CB_QR * COL) // kw
    k_cb = (CB_KR * COL) // kw
    v_cb = OFF_VR // vw
    z_cb = OFF_ZR // vw
    in_specs = [
        pl.BlockSpec((2 * R_HEADS, LANES), lambda b, h: (0, 0)),
        pl.BlockSpec((1, vw), lambda b, h: (0, h)),
        pl.BlockSpec((None, RET_BLOCK, kw), lambda b, h: (b, 0, q_cb + h)),
        pl.BlockSpec((None, RET_BLOCK, kw), lambda b, h: (b, 0, k_cb + h)),
        pl.BlockSpec((None, RET_BLOCK, vw), lambda b, h: (b, 0, v_cb + h)),
        pl.BlockSpec((None, RET_BLOCK, vw), lambda b, h: (b, 0, z_cb + h)),
    ]
    args = [dl_tile, gn, p3, p3, p3, p3]
    if states is not None:
        st = pl.BlockSpec((None, hps, R_KEY_DIM, R_VAL_DIM), lambda b, h: (b, h, 0, 0))
        in_specs += [st, st]
        args += list(states)
    n_chunks = RET_BLOCK // RET_CHUNK
    out = pl.pallas_call(
        functools.partial(_ret_kernel, has_state=states is not None),
        grid=(nb, R_HEADS // hps),
        in_specs=in_specs,
        out_specs=pl.BlockSpec((None, RET_BLOCK, vw), lambda b, h: (b, 0, h)),
        out_shape=jax.ShapeDtypeStruct((nb, RET_BLOCK, R_V_WIDTH), BF16),
        scratch_shapes=[
            pltpu.VMEM((hps, R_KEY_DIM, R_VAL_DIM), F32),
            pltpu.VMEM((hps, R_KEY_DIM, R_VAL_DIM), F32),
            pltpu.VMEM((hps, n_chunks, 2 * R_KEY_DIM, R_VAL_DIM), BF16),
        ],
        compiler_params=_params(("parallel", "parallel")),
        name="retention",
    )(*args)
    return out.reshape(nb * RET_BLOCK, R_V_WIDTH)


def _out_kernel(x_ref, mod_ref, o1_ref, o4_ref, o16_ref, l1_ref, l4_ref, l16_ref, za_ref, ur_ref, ga_ref, gr_ref,
                wa_ref, wb_ref, wo_ref, y_ref, o_scr, l_scr, yr_scr, o_mid, l_mid):
    tm = x_ref.shape[0]
    yr_scr[...] = jnp.dot(ur_ref[...], wb_ref[...], preferred_element_type=F32)
    def packed(o_ref, r, pair):
        a = o_ref[r, :, 2 * pair * LANES:(2 * pair + 1) * LANES].astype(F32)
        b = o_ref[r, :, (2 * pair + 1) * LANES:(2 * pair + 2) * LANES].astype(F32)
        return lax.bitcast_convert_type(a, jnp.uint32) | (lax.bitcast_convert_type(b, jnp.uint32) >> 16)

    n4 = tm // 4
    n16 = tm // 16
    for r in range(4):
        l_scr[0, pl.ds(r, n4, stride=4), :] = l4_ref[r]
        for pair in range(N_CHUNK // 2):
            o_scr[0, pair, pl.ds(r, n4, stride=4), :] = packed(o4_ref, r, pair)
    for r4 in range(4):
        for r in range(4):
            mid = pl.ds(r4 * n4 + r, n16, stride=4)
            l_mid[mid, :] = l16_ref[4 * r + r4]
            for pair in range(N_CHUNK // 2):
                o_mid[pair, mid, :] = packed(o16_ref, 4 * r + r4, pair)
    for r4 in range(4):
        l_scr[1, pl.ds(r4, n4, stride=4), :] = l_mid[r4 * n4:(r4 + 1) * n4, :]
        for pair in range(N_CHUNK // 2):
            o_scr[1, pair, pl.ds(r4, n4, stride=4), :] = o_mid[pair, r4 * n4:(r4 + 1) * n4, :]

    lane = lax.broadcasted_iota(jnp.int32, (OUT_SUB, LANES), 1)
    first_head = lane < A_HEAD_DIM
    for s in range(tm // OUT_SUB):
        rows = slice(s * OUT_SUB, (s + 1) * OUT_SUB)
        stats = (l1_ref[rows, :], l_scr[0, rows, :], l_scr[1, rows, :])
        top = jnp.maximum(jnp.maximum(stats[0], stats[1]), stats[2])
        es = [jnp.exp2(v - top) for v in stats]
        dens = [pltpu.roll(v, LANES - A_HEADS, 1) for v in stats]
        total = es[0] * dens[0] + es[1] * dens[1] + es[2] * dens[2]
        total = jnp.where(lane < A_HEADS, total, 1.0)
        ws = [e / total for e in es]
        ua = []
        for c in range(N_CHUNK):
            cols = slice(c * LANES, (c + 1) * LANES)
            words = (o_scr[0, c // 2, rows, :], o_scr[1, c // 2, rows, :])
            if c % 2 == 0:
                halves = [w & jnp.uint32(0xFFFF0000) for w in words]
            else:
                halves = [w << 16 for w in words]
            outs = (o1_ref[rows, cols].astype(F32),
                    lax.bitcast_convert_type(halves[0], F32), lax.bitcast_convert_type(halves[1], F32))
            ya = jnp.zeros((OUT_SUB, LANES), F32)
            for g in range(3):
                w_pair = jnp.where(first_head, ws[g][:, 2 * c:2 * c + 1], ws[g][:, 2 * c + 1:2 * c + 2])
                ya = ya + w_pair * outs[g]
            ua.append(_times_silu_of_half(ya, za_ref[rows, cols].astype(F32)).astype(BF16))
        ua = jnp.concatenate(ua, axis=-1)
        ya_p = jnp.dot(ua, wa_ref[...], preferred_element_type=F32)
        yr_p = yr_scr[rows, :]
        ta = jnp.tanh(ga_ref[rows, :].astype(F32))
        tr = jnp.tanh(gr_ref[rows, :].astype(F32))
        merged2 = (ya_p + ya_p * ta) + (yr_p + yr_p * tr)
        out = jnp.dot(merged2.astype(BF16), wo_ref[...], preferred_element_type=F32)
        y_ref[rows, :] = x_ref[rows, :] + (0.5 * mod_ref[2:3, :]) * out


def _output(x2, mod3, layer, b_off, seq, attn, p2, ur, wa, wb, wo):
    t = x2.shape[0]
    per_seq = seq // TM_OUT
    const = dict(pipeline_mode=pl.Buffered(1))
    (o1, l1), (o4, l4), (o16, l16) = attn

    def rows(width, cb=0):
        return pl.BlockSpec((TM_OUT, width), lambda i: (i, cb))

    def dilated(dil, width):
        return pl.BlockSpec((None, dil, TM_OUT // dil, width), lambda i: (i // per_seq, 0, i % per_seq, 0))

    return pl.pallas_call(
        _out_kernel,
        grid=(t // TM_OUT,),
        in_specs=[
            rows(D_MODEL),
            pl.BlockSpec((None, None, 3, D_MODEL), lambda i: (layer, i // per_seq + b_off, 0, 0)),
            rows(A_WIDTH), dilated(4, A_WIDTH), dilated(16, A_WIDTH),
            rows(LANES), dilated(4, LANES), dilated(16, LANES),
            rows(A_WIDTH, CB_ZA),
            rows(R_V_WIDTH),
            rows(D_MODEL, OFF_GA // D_MODEL),
            rows(D_MODEL, OFF_GR // D_MODEL),
            pl.BlockSpec((A_WIDTH, D_MODEL), lambda i: (0, 0), **const),
            pl.BlockSpec((R_V_WIDTH, D_MODEL), lambda i: (0, 0), **const),
            pl.BlockSpec((D_MODEL, D_MODEL), lambda i: (0, 0), **const),
        ],
        out_specs=rows(D_MODEL),
        out_shape=jax.ShapeDtypeStruct((t, D_MODEL), F32),
        scratch_shapes=[
            pltpu.VMEM((2, N_CHUNK // 2, TM_OUT, LANES), jnp.uint32),
            pltpu.VMEM((2, TM_OUT, LANES), F32),
            pltpu.VMEM((TM_OUT, D_MODEL), F32),
            pltpu.VMEM((N_CHUNK // 2, TM_OUT, LANES), jnp.uint32),
            pltpu.VMEM((TM_OUT, LANES), F32),
        ],
        compiler_params=_params(("parallel",), vmem=VMEM_LIMIT_OUT),
        name="merge_out_proj",
    )(x2, mod3, o1.reshape(t, A_WIDTH), o4, o16, l1.reshape(t, LANES), l4, l16, p2, ur, p2, p2, wa, wb, wo)


def _mixer_layer(x2, batch, seq, mod3, layer, b_off, lw, tables, e_bd):
    p2, a4, a16 = _inproj(x2, batch, seq, mod3, layer, b_off, lw["norm_g"], lw["w_in"], lw["gq"], lw["gk"], e_bd,
                          tables)
    attn = [_attention(qkv, dil) for qkv, dil in zip((p2.reshape(batch, 1, seq, IN_WIDTH), a4, a16), DILATIONS)]
    nb = batch * seq // RET_BLOCK
    pr = p2.reshape(nb, RET_BLOCK, IN_WIDTH)
    states = _retention_states(pr, lw["dl"]) if seq > RET_BLOCK else None
    ur = _retention(pr, lw["dl"], lw["gn"], states)
    return _output(x2, mod3, layer, b_off, seq, attn, p2, ur, lw["wa"], lw["wb"], lw["wo"])


def _half_gate_columns():
    col = jnp.arange(IN_WIDTH)
    za = (col >= CB_ZA * COL) & (col < (CB_ZA + 1) * COL)
    return jnp.where(za | (col >= OFF_ZR), 0.5, 1.0).astype(F32)[None, :]


def _layer_weights(layer,norm_g, w_in, q_norm_g, k_norm_g, ret_decay_logit, ret_norm_g, w_proj_a, w_proj_b, w_out):
    return {
        "norm_g": norm_g[layer].reshape(1, D_MODEL),
        "w_in": (w_in[layer] * _half_gate_columns()).astype(BF16),
        "gq": (jnp.tile(q_norm_g[layer], A_HEADS) * (A_HEAD_DIM ** -0.5 * LOG2E)).reshape(1, A_WIDTH),
        "gk": jnp.tile(k_norm_g[layer], A_HEADS).reshape(1, A_WIDTH),
        "dl": jnp.broadcast_to(ret_decay_logit[layer].astype(F32).reshape(2 * R_HEADS, 1), (2 * R_HEADS, LANES)),
        "gn": ret_norm_g[layer].reshape(1, R_V_WIDTH),
        "wa": w_proj_a[layer].astype(BF16),
        "wb": w_proj_b[layer].astype(BF16),
        "wo": w_out[layer].astype(BF16),
    }


def kernel(x_prompt, x_sample, c_prompt, c_sample, norm_g, w_ada, b_ada, w_in, q_norm_g, k_norm_g, ret_decay_logit,
           ret_norm_g, w_proj_a, w_proj_b, w_out):
    bp, sp, _ = x_prompt.shape
    bs, ss, _ = x_sample.shape
    pad = (-(bp + bs)) % 8
    c_all = jnp.concatenate([c_prompt, c_sample, jnp.zeros((pad, D_MODEL), F32)], axis=0)
    mod3 = _modulation(c_all, w_ada, b_ada).reshape(DEPTH, bp + bs + pad, 3, D_MODEL)
    head_of_lane = jnp.arange(E_BLK) // A_HEAD_DIM
    e_bd = (head_of_lane[:, None] == head_of_lane[None, :]).astype(BF16)
    tab_p = _rope_tables(sp)
    tab_s = tab_p if ss == sp else _rope_tables(ss)
    yp = x_prompt.reshape(bp * sp, D_MODEL)
    ys = x_sample.reshape(bs * ss, D_MODEL)
    for layer in range(DEPTH):
        lw = _layer_weights(layer, norm_g, w_in, q_norm_g, k_norm_g, ret_decay_logit, ret_norm_g, w_proj_a, w_proj_b,
                            w_out)
        yp = _mixer_layer(yp, bp, sp, mod3, layer, 0, lw, tab_p, e_bd)
        ys = _mixer_layer(ys, bs, ss, mod3, layer, bp, lw, tab_s, e_bd)
    return yp.reshape(bp, sp, D_MODEL), ys.reshape(bs, ss, D_MODEL)
```
